```python
import math
import jax
import jax.numpy as jnp
from jax import lax
import numpy as np

D_MODEL = 1024
BATCH = 8
SEQ = 2048
DEPTH = 4

GRID_W = 64
CTX_LEN = 256

RWKV_WIDTH = D_MODEL // 2
RWKV_HEAD = 64
RWKV_HEADS = RWKV_WIDTH // RWKV_HEAD
DECAY_LORA = 64
ICLR_LORA = 64
GATE_LORA = 128
CONV_WIDTH = D_MODEL - RWKV_WIDTH
CONV_K = 3
LNX_EPS = 64e-5

DIFF_HEADS = 4
DIFF_HEAD = 64
DIFF_VHEAD = 2 * DIFF_HEAD
DIFF_WIDTH = DIFF_HEADS * DIFF_VHEAD
NA_HEADS = 8
NA_HEAD = 64
NA_WIDTH = NA_HEADS * NA_HEAD
NA_MAX_ROWS = 8
NA_COLS = 16
ROPE_BASE = 10000.0
Q_BLOCK = 128
SUBLN_EPS = 1e-5

N_GROUPS = 4
EXPERTS_PER_GROUP = 4
N_EXPERTS = N_GROUPS * EXPERTS_PER_GROUP
TOP_K_IN_GROUP = 2
D_EXPERT = 512
RMS_EPS = 1e-6

N_EVEN = (DEPTH + 1) // 2
N_ODD = DEPTH // 2
EVEN_WIDTHS = (RWKV_WIDTH, RWKV_WIDTH, RWKV_WIDTH, DECAY_LORA, DECAY_LORA, ICLR_LORA, ICLR_LORA, GATE_LORA, CONV_WIDTH, CONV_WIDTH, CONV_WIDTH)
EVEN_PROJ = sum(EVEN_WIDTHS)
ODD_WIDTHS = (DIFF_HEADS * 2 * DIFF_HEAD, DIFF_HEADS * 2 * DIFF_HEAD, DIFF_WIDTH, NA_WIDTH, NA_WIDTH, NA_WIDTH)
ODD_PROJ = sum(ODD_WIDTHS)

kernel_name = 'hybrid_rwkv7_shortconv_diffattn_natten_hmoe'

F32 = jnp.float32


def split_cols(p, widths):
    return jnp.split(p, np.cumsum(widths)[:-1].tolist(), axis=-1)


def rmsnorm(x, g, eps=RMS_EPS):
    xf = x.astype(F32)
    y = xf * lax.rsqrt(jnp.mean(xf * xf, axis=-1, keepdims=True) + eps)
    return (y * g.astype(F32)).astype(x.dtype)


def modulate(x, g, shift, scale):
    return rmsnorm(x, g) * (1.0 + scale) + shift


def ada_mod(cond, w, b):
    return jnp.split(jax.nn.silu(cond) @ w + b, 6, axis=-1)


def heads(t, n_heads, head_dim):
    return t.reshape(t.shape[:-1] + (n_heads, head_dim))


def rwkv7_scan(r, decay, k, v, kk, a, s0, reverse):
    def step(S, inp):
        r_t, w_t, k_t, v_t, kk_t, a_t = inp
        s_kk = jnp.einsum('bhvk,bhk->bhv', S, kk_t)
        S = (S * w_t[:, :, None, :] - s_kk[..., None] * (kk_t * a_t)[:, :, None, :]
             + v_t[..., None] * k_t[:, :, None, :])
        return S, jnp.einsum('bhvk,bhk->bhv', S, r_t)
    xs = tuple(jnp.moveaxis(t, 1, 0) for t in (r, decay, k, v, kk, a))
    s_final, ys = lax.scan(step, s0, xs, reverse=reverse)
    return jnp.moveaxis(ys, 0, 1), s_final


def rwkv7_bidir_scan(r, k, v, wd, ad, w0, w_up, a0, a_up, k_k, k_a, s0):
    hd = lambda t: heads(t, RWKV_HEADS, RWKV_HEAD)
    kk = hd(k * k_k)
    kk = kk * lax.rsqrt(jnp.maximum(jnp.sum(kk * kk, axis=-1, keepdims=True), 1e-24))
    ys, finals = [], []
    for d in range(2):
        w = w0[d] + jnp.tanh(wd[d]) @ w_up[d]
        decay = jnp.exp(-jnp.exp(-jax.nn.softplus(-w) - 0.5))
        a = jax.nn.sigmoid(a0[d] + ad[d] @ a_up[d])
        k_d = k * (1.0 + (a - 1.0) * k_a)
        y, s = rwkv7_scan(hd(r), hd(decay), hd(k_d), hd(v), kk, hd(a), s0[d], reverse=(d == 1))
        ys.append(y)
        finals.append(s)
    return ys[0] + ys[1], (finals[0], finals[1])


def rwkv7_readout(y, r, k, v, gd, g_up, r_k, lnx_w, lnx_b):
    hd = lambda t: heads(t, RWKV_HEADS, RWKV_HEAD)
    B, L = y.shape[:2]
    yc = y - jnp.mean(y, axis=-1, keepdims=True)
    yn = yc * lax.rsqrt(jnp.mean(yc * yc, axis=-1, keepdims=True) + LNX_EPS)
    out = yn.reshape(B, L, RWKV_WIDTH) * lnx_w + lnx_b
    bonus = jnp.sum(hd(r * k) * r_k, axis=-1, keepdims=True) * hd(v)
    out = out + bonus.reshape(B, L, RWKV_WIDTH)
    g = jax.nn.sigmoid(gd) @ g_up
    return out * g


def short_conv(b_gate, c_gate, x_in, conv_w):
    L = x_in.shape[1]
    u = jnp.pad(c_gate * x_in, ((0, 0), (CONV_K // 2, CONV_K // 2), (0, 0)))
    y = sum(u[:, j:j + L] * conv_w[j] for j in range(CONV_K))
    return b_gate * y


def even_mixer(pc, px, w0, w_up, a0, a_up, g_up, k_k, k_a, r_k, lnx_w, lnx_b, conv_w, need_ctx):
    c_parts = split_cols(pc.astype(F32), EVEN_WIDTHS)
    x_parts = split_cols(px.astype(F32), EVEN_WIDTHS)
    B = px.shape[0]
    zeros = jnp.zeros((B, RWKV_HEADS, RWKV_HEAD, RWKV_HEAD), F32)

    def scan_stream(p, s0):
        r, k, v, wd_f, wd_b, ad_f, ad_b = p[:7]
        return rwkv7_bidir_scan(r, k, v, (wd_f, wd_b), (ad_f, ad_b), w0, w_up, a0, a_up, k_k, k_a, s0)

    def finish(p, y, dtype):
        r, k, v, gd = p[0], p[1], p[2], p[7]
        h_a = rwkv7_readout(y, r, k, v, gd, g_up, r_k, lnx_w, lnx_b)
        h_b = short_conv(p[8], p[9], p[10], conv_w)
        return jnp.concatenate([h_a, h_b], axis=-1).astype(dtype)

    y_c, s_ctx = scan_stream(c_parts, (zeros, zeros))
    y_x, _ = scan_stream(x_parts, s_ctx)
    h_x = finish(x_parts, y_x, px.dtype)
    h_c = finish(c_parts, y_c, pc.dtype) if need_ctx else None
    return h_x, h_c


def axial_rope_tables(n_tokens, head_dim):
    n_freq = head_dim // 4
    inv_freq = ROPE_BASE ** (-jnp.arange(n_freq, dtype=F32) / n_freq)
    t = jnp.arange(n_tokens, dtype=jnp.int32)
    pos = jnp.stack([t // GRID_W, t % GRID_W], axis=-1).astype(F32)
    ang = pos[:, :, None] * inv_freq
    return jnp.cos(ang), jnp.sin(ang)


def apply_axial_rope(x, cos, sin):
    shp = x.shape
    xs = x.astype(F32).reshape(shp[:-1] + (2, 2, shp[-1] // 4))
    x1, x2 = xs[..., 0, :], xs[..., 1, :]
    out = jnp.stack([x1 * cos - x2 * sin, x2 * cos + x1 * sin], axis=-2)
    return out.reshape(shp).astype(x.dtype)


def diff_attend(q, k, v, lam):
    s = jnp.einsum('bqhmd,bkhmd->bhmqk', q, k).astype(F32) * (DIFF_HEAD ** -0.5)
    p = jax.nn.softmax(s, axis=-1)
    w = p[:, :, 0] - lam * p[:, :, 1]
    return jnp.einsum('bhqk,bkhe->bqhe', w, v.astype(F32))


def dense_attend(q, k, v):
    s = jnp.einsum('bqhd,bkhd->bhqk', q, k).astype(F32) * (q.shape[-1] ** -0.5)
    return jnp.einsum('bhqk,bkhd->bqhd', jax.nn.softmax(s, axis=-1), v.astype(F32))


def na_latent(q, k, v, k_ctx, v_ctx, rel_bias, rows):
    B, L, H, d = q.shape
    kr = min(NA_MAX_ROWS, rows)
    n_ctx = k_ctx.shape[1]
    scale = d ** -0.5
    k_grid = k.reshape(B, rows, GRID_W, H, d)
    v_grid = v.reshape(B, rows, GRID_W, H, d)
    cols = jnp.arange(GRID_W)
    col_start = jnp.clip(cols - NA_COLS // 2, 0, GRID_W - NA_COLS)
    col_idx = col_start[:, None] + jnp.arange(NA_COLS)[None, :]
    col_off = col_idx - cols[:, None] + (NA_COLS - 1)

    def row_block(args):
        r, q_row = args
        r_start = jnp.clip(r - kr // 2, 0, rows - kr)
        k_win = lax.dynamic_slice_in_dim(k_grid, r_start, kr, axis=1)[:, :, col_idx]
        v_win = lax.dynamic_slice_in_dim(v_grid, r_start, kr, axis=1)[:, :, col_idx]
        row_off = r_start + jnp.arange(kr) - r + (NA_MAX_ROWS - 1)
        bias = rel_bias[:, row_off[None, :, None], col_off[:, None, :]]
        s_loc = jnp.einsum('bchd,brcjhd->bhcrj', q_row, k_win).astype(F32) * scale + bias
        s_ctx = jnp.einsum('bchd,bkhd->bhck', q_row, k_ctx).astype(F32) * scale
        p = jax.nn.softmax(jnp.concatenate([s_ctx, s_loc.reshape(B, H, GRID_W, kr * NA_COLS)], axis=-1), axis=-1)
        p_ctx = p[..., :n_ctx]
        p_loc = p[..., n_ctx:].reshape(B, H, GRID_W, kr, NA_COLS)
        return (jnp.einsum('bhck,bkhd->bchd', p_ctx, v_ctx.astype(F32))
                + jnp.einsum('bhcrj,brcjhd->bchd', p_loc, v_win.astype(F32)))

    q_rows = jnp.moveaxis(q.reshape(B, rows, GRID_W, H, d), 1, 0)
    out = lax.map(row_block, (jnp.arange(rows), q_rows))
    return jnp.moveaxis(out, 0, 1).reshape(B, L, H * d)


def odd_mixer(pc, px, lam, subln, rel_bias, rows, layer_idx, need_ctx):
    B, L, _ = px.shape
    Lc = pc.shape[1]
    dq_c, dk_c, dv_c, nq_c, nk_c, nv_c = split_cols(pc, ODD_WIDTHS)
    dq_x, dk_x, dv_x, nq_x, nk_x, nv_x = split_cols(px, ODD_WIDTHS)
    qk_shape = lambda t: t.reshape(t.shape[:2] + (DIFF_HEADS, 2, DIFF_HEAD))
    v_shape = lambda t: t.reshape(t.shape[:2] + (DIFF_HEADS, DIFF_VHEAD))
    na_shape = lambda t: t.reshape(t.shape[:2] + (NA_HEADS, NA_HEAD))

    lam_init = 0.8 - 0.6 * math.exp(-0.3 * layer_idx)
    lam32 = lam.astype(F32)
    lam_full = jnp.exp(jnp.sum(lam32[0] * lam32[1])) - jnp.exp(jnp.sum(lam32[2] * lam32[3])) + lam_init

    cos, sin = axial_rope_tables(L, DIFF_HEAD)
    cos, sin = cos[:, None, None], sin[:, None, None]
    q_x = apply_axial_rope(qk_shape(dq_x), cos, sin)
    k_all = jnp.concatenate([qk_shape(dk_c), apply_axial_rope(qk_shape(dk_x), cos, sin)], axis=1)
    v_all = jnp.concatenate([v_shape(dv_c), v_shape(dv_x)], axis=1)
    n_blk = L // Q_BLOCK
    q_blocks = jnp.moveaxis(q_x.reshape(B, n_blk, Q_BLOCK, DIFF_HEADS, 2, DIFF_HEAD), 1, 0)
    o = lax.map(lambda qb: diff_attend(qb, k_all, v_all, lam_full), q_blocks)
    o = jnp.moveaxis(o, 0, 1).reshape(B, L, DIFF_HEADS, DIFF_VHEAD)

    def diff_out(t):
        return (rmsnorm(t, subln, SUBLN_EPS) * (1.0 - lam_init)).reshape(t.shape[:2] + (DIFF_WIDTH,))

    na_x = na_latent(na_shape(nq_x), na_shape(nk_x), na_shape(nv_x), na_shape(nk_c), na_shape(nv_c), rel_bias, rows)
    h_x = jnp.concatenate([diff_out(o), na_x], axis=-1).astype(px.dtype)
    h_c = None
    if need_ctx:
        o_c = diff_attend(qk_shape(dq_c), qk_shape(dk_c), v_shape(dv_c), lam_full)
        na_c = dense_attend(na_shape(nq_c), na_shape(nk_c), na_shape(nv_c))
        h_c = jnp.concatenate([diff_out(o_c), na_c.reshape(B, Lc, NA_WIDTH)], axis=-1).astype(pc.dtype)
    return h_x, h_c


def hier_moe(h, wg, bg, we, be, w1, w3, w2):
    T, D = h.shape
    pg = jax.nn.softmax((h @ wg + bg).astype(F32), axis=-1)
    pg_top, g_sel = lax.top_k(pg, 1)
    le = (h @ we + be).astype(F32).reshape(T, N_GROUPS, EXPERTS_PER_GROUP)
    le_g = jnp.take_along_axis(le, g_sel[:, :, None], axis=1)[:, 0]
    pe_top, e_sel = lax.top_k(jax.nn.softmax(le_g, axis=-1), TOP_K_IN_GROUP)
    wts = pg_top * pe_top / jnp.sum(pe_top, axis=-1, keepdims=True)
    eid = g_sel * EXPERTS_PER_GROUP + e_sel
    combine = jnp.sum(jax.nn.one_hot(eid, N_EXPERTS, dtype=F32) * wts[..., None], axis=1)
    out = jnp.zeros((T, D), F32)
    for e in range(N_EXPERTS):
        y = (jax.nn.silu(h @ w1[e]) * (h @ w3[e])) @ w2[e]
        out = out + combine[:, e:e + 1] * y
    return out.astype(h.dtype)


def setup_inputs(seed: int = 0) -> dict:
    key = jax.random.key(seed)
    ks = iter(jax.random.split(key, 40))

    def nrm(shape, scale):
        return jax.random.normal(next(ks), shape, F32) * scale

    D = D_MODEL
    C = RWKV_WIDTH
    inv = D ** -0.5
    return {
        'x': nrm((BATCH, SEQ, D), 1.0),
        'c': nrm((BATCH, D), 1.0),
        'ctx': nrm((BATCH, CTX_LEN, D), 1.0),
        'c_ctx': nrm((D,), 1.0),
        'ada_w': nrm((DEPTH, D, 6 * D), 0.5 * inv),
        'ada_b': nrm((DEPTH, 6 * D), 0.02),
        'norm_g': 1.0 + nrm((DEPTH, 2, D), 0.02),
        'final_g': 1.0 + nrm((D,), 0.02),
        'ev_w_in': nrm((N_EVEN, D, EVEN_PROJ), inv),
        'ev_w_out': nrm((N_EVEN, D, D), inv),
        'ev_decay_w0': nrm((N_EVEN, 2, C), 1.0),
        'ev_decay_up': nrm((N_EVEN, 2, DECAY_LORA, C), DECAY_LORA ** -0.5),
        'ev_iclr_a0': nrm((N_EVEN, 2, C), 0.5),
        'ev_iclr_up': nrm((N_EVEN, 2, ICLR_LORA, C), 0.5 * ICLR_LORA ** -0.5),
        'ev_gate_up': nrm((N_EVEN, GATE_LORA, C), GATE_LORA ** -0.5),
        'ev_k_k': 0.85 + nrm((N_EVEN, C), 0.05),
        'ev_k_a': 1.0 + nrm((N_EVEN, C), 0.05),
        'ev_r_k': nrm((N_EVEN, RWKV_HEADS, RWKV_HEAD), 0.1),
        'ev_lnx_w': 1.0 + nrm((N_EVEN, C), 0.02),
        'ev_lnx_b': nrm((N_EVEN, C), 0.02),
        'ev_conv_w': nrm((N_EVEN, CONV_K, CONV_WIDTH), CONV_K ** -0.5),
        'od_w_in': nrm((N_ODD, D, ODD_PROJ), inv),
        'od_w_out': nrm((N_ODD, D, D), inv),
        'od_lambda': nrm((N_ODD, 4, DIFF_HEAD), 0.1),
        'od_subln': 1.0 + nrm((N_ODD, DIFF_VHEAD), 0.02),
        'od_rel_bias': nrm((N_ODD, NA_HEADS, 2 * NA_MAX_ROWS - 1, 2 * NA_COLS - 1), 0.2),
        'moe_wg': nrm((DEPTH, D, N_GROUPS), inv),
        'moe_bg': nrm((DEPTH, N_GROUPS), 0.01),
        'moe_we': nrm((DEPTH, D, N_EXPERTS), inv),
        'moe_be': nrm((DEPTH, N_EXPERTS), 0.01),
        'moe_w1': nrm((DEPTH, N_EXPERTS, D, D_EXPERT), inv),
        'moe_w3': nrm((DEPTH, N_EXPERTS, D, D_EXPERT), inv),
        'moe_w2': nrm((DEPTH, N_EXPERTS, D_EXPERT, D), D_EXPERT ** -0.5),
    }


def reference(x, c, ctx, c_ctx, ada_w, ada_b, norm_g, final_g,
              ev_w_in, ev_w_out, ev_decay_w0, ev_decay_up, ev_iclr_a0, ev_iclr_up, ev_gate_up,
              ev_k_k, ev_k_a, ev_r_k, ev_lnx_w, ev_lnx_b, ev_conv_w,
              od_w_in, od_w_out, od_lambda, od_subln, od_rel_bias,
              moe_wg, moe_bg, moe_we, moe_be, moe_w1, moe_w3, moe_w2):
    B, L, D = x.shape
    Lc = ctx.shape[1]
    rows = L // GRID_W
    xl, xc = x, ctx
    for l in range(DEPTH):
        need_ctx = l < DEPTH - 1
        i = l // 2
        sh1_x, sc1_x, g1_x, sh2_x, sc2_x, g2_x = [m[:, None, :] for m in ada_mod(c, ada_w[l], ada_b[l])]
        sh1_c, sc1_c, g1_c, sh2_c, sc2_c, g2_c = ada_mod(c_ctx, ada_w[l], ada_b[l])
        hx = modulate(xl, norm_g[l, 0], sh1_x, sc1_x)
        hc = modulate(xc, norm_g[l, 0], sh1_c, sc1_c)
        if l % 2 == 0:
            ox, oc = even_mixer(hc @ ev_w_in[i], hx @ ev_w_in[i], ev_decay_w0[i], ev_decay_up[i],
                                ev_iclr_a0[i], ev_iclr_up[i], ev_gate_up[i], ev_k_k[i], ev_k_a[i],
                                ev_r_k[i], ev_lnx_w[i], ev_lnx_b[i], ev_conv_w[i], need_ctx)
            w_out = ev_w_out[i]
        else:
            ox, oc = odd_mixer(hc @ od_w_in[i], hx @ od_w_in[i], od_lambda[i], od_subln[i],
                               od_rel_bias[i], rows, l, need_ctx)
            w_out = od_w_out[i]
        xl = xl + g1_x * (ox @ w_out)
        hx = modulate(xl, norm_g[l, 1], sh2_x, sc2_x).reshape(B * L, D)
        if need_ctx:
            xc = xc + g1_c * (oc @ w_out)
            hc = modulate(xc, norm_g[l, 1], sh2_c, sc2_c).reshape(B * Lc, D)
            y = hier_moe(jnp.concatenate([hx, hc], axis=0), moe_wg[l], moe_bg[l], moe_we[l], moe_be[l],
                         moe_w1[l], moe_w3[l], moe_w2[l])
            xc = xc + g2_c * y[B * L:].reshape(B, Lc, D)
            y_x = y[:B * L]
        else:
            y_x = hier_moe(hx, moe_wg[l], moe_bg[l], moe_we[l], moe_be[l], moe_w1[l], moe_w3[l], moe_w2[l])
        xl = xl + g2_x * y_x.reshape(B, L, D)
    return rmsnorm(xl, final_g)
```

```python
import functools
import math

import jax
import jax.numpy as jnp
from jax import lax
from jax.experimental import pallas as pl
from jax.experimental.pallas import tpu as pltpu

F32 = jnp.float32
BF16 = jnp.bfloat16
HI = lax.Precision.HIGHEST

D_MODEL = 1024
DEPTH = 4
GRID_W = 64
CTX_LEN = 256

RWKV_WIDTH = 512
RWKV_HEAD = 64
RWKV_HEADS = 8
DECAY_LORA = 64
ICLR_LORA = 64
GATE_LORA = 128
CONV_WIDTH = 512
LNX_EPS = 64e-5

DIFF_HEADS = 4
DIFF_HEAD = 64
DIFF_VHEAD = 128
DIFF_WIDTH = 512
NA_HEADS = 8
NA_HEAD = 64
NA_WIDTH = 512
NA_MAX_ROWS = 8
NA_COLS = 16
ROPE_BASE = 10000.0
SUBLN_EPS = 1e-5

N_GROUPS = 4
EXPERTS_PER_GROUP = 4
N_EXPERTS = 16
D_EXPERT = 512
RMS_EPS = 1e-6

EVEN_PROJ = 3456
ODD_PROJ = 3072

LANES = 128
TOKEN_TILE = 256
SCAN_TB = 128
SCAN_NB = 2
SCAN_GROUP = 8
VMEM_LIMIT = 56 * 1024 * 1024
NEG_BIG = -1e30


def _cp(sem, vmem=VMEM_LIMIT):
    return pltpu.CompilerParams(dimension_semantics=sem, vmem_limit_bytes=vmem)


def _dot(a, b):
    return jnp.dot(a, b, preferred_element_type=F32)


def _dot_hi(a, b):
    return jnp.dot(a, b, preferred_element_type=F32, precision=HI)


def _dot_nt(a, b):
    return lax.dot_general(a, b, (((1,), (1,)), ((), ())), preferred_element_type=F32)


def _head_ones(width, head):
    i = lax.broadcasted_iota(jnp.int32, (width, width), 0) // head
    j = lax.broadcasted_iota(jnp.int32, (width, width), 1) // head
    return (i == j).astype(F32)


def _ada_kernel(cond_ref, w_ref, b_ref, o_ref):
    cond = cond_ref[...]
    s = cond * jax.nn.sigmoid(cond)
    o_ref[0] = _dot_hi(s, w_ref[0]) + b_ref[0]


def ada_all(cond, ada_w, ada_b):
    depth, d, n = ada_w.shape
    tn = 1536
    return pl.pallas_call(
        _ada_kernel,
        grid=(depth, n // tn),
        in_specs=[pl.BlockSpec((16, d), lambda l, j: (0, 0)),
                  pl.BlockSpec((1, d, tn), lambda l, j: (l, 0, j)),
                  pl.BlockSpec((1, 1, tn), lambda l, j: (l, 0, j))],
        out_specs=pl.BlockSpec((1, 16, tn), lambda l, j: (l, 0, j)),
        out_shape=jax.ShapeDtypeStruct((depth, 16, n), F32),
        compiler_params=_cp(("parallel", "parallel")),
        name="ada_mod",
    )(cond, ada_w, ada_b.reshape(depth, 1, n))


def _mod_index(n_batch):
    return lambda b, i: (jnp.where(i == 0, n_batch, b), 0, 0)


def _norm_mod(x, g, shift, scale):
    ms = jnp.mean(x * x, axis=-1, keepdims=True)
    return x * lax.rsqrt(ms + RMS_EPS) * g * (1.0 + scale) + shift


def _inproj_even_kernel(x_ref, mod_ref, g_ref, w_ref, wvt_ref,
                        rkv_ref, lora_ref, gd_ref, conv_ref, vt_ref):
    m = mod_ref[0]
    h = _norm_mod(x_ref[0], g_ref[...], m[0:1], m[1:2]).astype(BF16)
    p = _dot(h, w_ref[...])
    rkv_ref[0] = p[:, 0:1536]
    lora_ref[0] = p[:, 1536:1792]
    gd_ref[0] = p[:, 1792:1920]
    conv_ref[0] = p[:, 1920:3456]
    vt_ref[0] = _dot_nt(wvt_ref[...], h)


def inproj_even(x, modl, g, w, wvt):
    nb, s, d = x.shape
    tm = TOKEN_TILE
    tok = lambda n: pl.BlockSpec((1, tm, n), lambda b, i: (b, i, 0))
    return pl.pallas_call(
        _inproj_even_kernel,
        grid=(nb, s // tm),
        in_specs=[tok(d),
                  pl.BlockSpec((1, 8, d), _mod_index(nb)),
                  pl.BlockSpec((1, d), lambda b, i: (0, 0)),
                  pl.BlockSpec((d, EVEN_PROJ), lambda b, i: (0, 0)),
                  pl.BlockSpec((RWKV_WIDTH, d), lambda b, i: (0, 0))],
        out_specs=[tok(1536), tok(256), tok(128), tok(1536),
                   pl.BlockSpec((1, RWKV_WIDTH, tm), lambda b, i: (b, 0, i))],
        out_shape=[jax.ShapeDtypeStruct((nb, s, 1536), F32),
                   jax.ShapeDtypeStruct((nb, s, 256), F32),
                   jax.ShapeDtypeStruct((nb, s, 128), F32),
                   jax.ShapeDtypeStruct((nb, s, 1536), F32),
                   jax.ShapeDtypeStruct((nb, RWKV_WIDTH, s), F32)],
        compiler_params=_cp(("parallel", "parallel")),
        name="inproj_even",
    )(x, modl, g, w, wvt)


def _rope(x, cos, sin_signed):
    lane = lax.broadcasted_iota(jnp.int32, x.shape, 1)
    partner = jnp.where((lane % 32) < 16, pltpu.roll(x, LANES - 16, 1), pltpu.roll(x, 16, 1))
    return x * cos + partner * sin_signed


def _inproj_odd_kernel(x_ref, mod_ref, g_ref, w_ref, cos_ref, sin_ref,
                       dq_ref, dk_ref, dv_ref, nq_ref, nk_ref, nv_ref):
    m = mod_ref[0]
    h = _norm_mod(x_ref[0], g_ref[...], m[0:1], m[1:2]).astype(BF16)
    p = _dot(h, w_ref[...])
    cos = cos_ref[...]
    sin = sin_ref[...]
    for out_ref, base in ((dq_ref, 0), (dk_ref, 512)):
        for c in range(4):
            blk = p[:, base + c * LANES: base + (c + 1) * LANES]
            out_ref[0, :, c * LANES:(c + 1) * LANES] = _rope(blk, cos, sin).astype(BF16)
    dv_ref[0] = p[:, 1024:1536].astype(BF16)
    nq_ref[0] = p[:, 1536:2048].astype(BF16)
    nk_ref[0] = p[:, 2048:2560].astype(BF16)
    nv_ref[0] = p[:, 2560:3072].astype(BF16)


def inproj_odd(x, modl, g, w, cos_t, sin_t):
    nb, s, d = x.shape
    tm = TOKEN_TILE
    tok = lambda n: pl.BlockSpec((1, tm, n), lambda b, i: (b, i, 0))
    tab = pl.BlockSpec((tm, LANES), lambda b, i: (i, 0))
    return pl.pallas_call(
        _inproj_odd_kernel,
        grid=(nb, s // tm),
        in_specs=[tok(d),
                  pl.BlockSpec((1, 8, d), _mod_index(nb)),
                  pl.BlockSpec((1, d), lambda b, i: (0, 0)),
                  pl.BlockSpec((d, ODD_PROJ), lambda b, i: (0, 0)),
                  tab, tab],
        out_specs=[tok(512)] * 6,
        out_shape=[jax.ShapeDtypeStruct((nb, s, 512), BF16)] * 6,
        compiler_params=_cp(("parallel", "parallel")),
        name="inproj_odd",
    )(x, modl, g, w, cos_t, sin_t)


def rope_tables(seq):
    n_freq = DIFF_HEAD // 4
    inv_freq = ROPE_BASE ** (-jnp.arange(n_freq, dtype=F32) / n_freq)
    t = jnp.arange(seq, dtype=jnp.int32)
    pos = jnp.stack([t // GRID_W, t % GRID_W], axis=-1).astype(F32)
    ang = pos[:, :, None] * inv_freq
    cos, sin = jnp.cos(ang), jnp.sin(ang)
    cos64 = jnp.concatenate([cos[:, 0], cos[:, 0], cos[:, 1], cos[:, 1]], axis=-1)
    sin64 = jnp.concatenate([-sin[:, 0], sin[:, 0], -sin[:, 1], sin[:, 1]], axis=-1)
    cos_t = jnp.concatenate([jnp.ones((CTX_LEN, 64), F32), cos64], axis=0)
    sin_t = jnp.concatenate([jnp.zeros((CTX_LEN, 64), F32), sin64], axis=0)
    return jnp.tile(cos_t, (1, 2)), jnp.tile(sin_t, (1, 2))


def _prep_kernel(rkv_ref, lora_ref, w0_ref, wup_ref, a0_ref, aup_ref, kk_w_ref, ka_ref,
                 kk_ref, dec_ref, bb_ref, kd_ref):
    k = rkv_ref[0]
    lora = lora_ref[0]
    ones = _head_ones(RWKV_WIDTH, RWKV_HEAD)
    kk = k * kk_w_ref[...]
    ss = _dot_hi(kk * kk, ones)
    kk = kk * lax.rsqrt(jnp.maximum(ss, 1e-24))
    kk_ref[0] = kk
    for d in range(2):
        wd = lora[:, d * 64:(d + 1) * 64]
        ad = lora[:, 128 + d * 64:128 + (d + 1) * 64]
        w = w0_ref[d:d + 1, :] + _dot_hi(jnp.tanh(wd), wup_ref[d])
        dec_ref[d, 0] = jnp.exp(-math.exp(-0.5) * jax.nn.sigmoid(w))
        a = jax.nn.sigmoid(a0_ref[d:d + 1, :] + _dot_hi(ad, aup_ref[d]))
        kd_ref[d, 0] = k * (1.0 + (a - 1.0) * ka_ref[...])
        bb_ref[d, 0] = kk * a


def rwkv_prep(rkv, lora, w0, wup, a0, aup, k_k, k_a):
    nb, s, _ = rkv.shape
    tm = TOKEN_TILE
    c = RWKV_WIDTH
    full = lambda shape: pl.BlockSpec(shape, lambda b, i: (0,) * len(shape))
    dir_out = pl.BlockSpec((2, 1, tm, c), lambda b, i: (0, b, i, 0))
    dir_shape = jax.ShapeDtypeStruct((2, nb, s, c), F32)
    return pl.pallas_call(
        _prep_kernel,
        grid=(nb, s // tm),
        in_specs=[pl.BlockSpec((1, tm, c), lambda b, i: (b, i, 1)),
                  pl.BlockSpec((1, tm, 256), lambda b, i: (b, i, 0)),
                  full((2, c)), full((2, DECAY_LORA, c)), full((2, c)), full((2, ICLR_LORA, c)),
                  full((1, c)), full((1, c))],
        out_specs=[pl.BlockSpec((1, tm, c), lambda b, i: (b, i, 0)), dir_out, dir_out, dir_out],
        out_shape=[jax.ShapeDtypeStruct((nb, s, c), F32), dir_shape, dir_shape, dir_shape],
        compiler_params=_cp(("parallel", "parallel")),
        name="rwkv_prep",
    )(rkv, lora, w0, wup, a0, aup, k_k.reshape(1, c), k_a.reshape(1, c))


def _scan_kernel(rf_ref, kkf_ref, wf_ref, bf_ref, kf_ref, vtf_ref,
                 rb_ref, kkb_ref, wb_ref, bb_ref, kb_ref, vtb_ref,
                 yf_ref, yb_ref,
                 s_scr, acc_scr, vsp_scr, vb_scr):
    nb, tb = SCAN_NB, SCAN_TB
    npair = RWKV_HEADS // 2
    j = pl.program_id(1)

    @pl.when(j == 0)
    def _():
        s_scr[...] = jnp.zeros_like(s_scr)

    acc_scr[...] = jnp.zeros_like(acc_scr)
    rows = ((rf_ref, kkf_ref, wf_ref, bf_ref, kf_ref), (rb_ref, kkb_ref, wb_ref, bb_ref, kb_ref))
    vts = (vtf_ref, vtb_ref)
    outs = (yf_ref, yb_ref)

    for bi in range(nb):
        for d in range(2):
            v = vts[d][bi]
            hi = v.astype(BF16)
            r1 = v - hi.astype(F32)
            mid = r1.astype(BF16)
            lo = (r1 - mid.astype(F32)).astype(BF16)
            vsp_scr[bi * 2 + d, 0] = hi
            vsp_scr[bi * 2 + d, 1] = mid
            vsp_scr[bi * 2 + d, 2] = lo

    lane = lax.broadcasted_iota(jnp.int32, (RWKV_HEAD, LANES), 1)
    low = lane < RWKV_HEAD
    sel_row = lax.broadcasted_iota(jnp.int32, (tb, SCAN_GROUP * LANES), 0)
    sel_col = lax.broadcasted_iota(jnp.int32, (tb, SCAN_GROUP * LANES), 1) // LANES

    def seg_sums(x):
        lo_s = jnp.sum(jnp.where(low, x, 0.0), axis=1, keepdims=True)
        hi_s = jnp.sum(jnp.where(low, 0.0, x), axis=1, keepdims=True)
        return lo_s, hi_s

    def group(g, carry):
        t0 = g * SCAN_GROUP
        onehot = ((sel_row == t0 + sel_col).astype(BF16),
                  (sel_row == tb - 1 - t0 - sel_col).astype(BF16))
        for bi in range(nb):
            for d in range(2):
                bd = bi * 2 + d
                vb = (_dot(vsp_scr[bd, 0], onehot[d]) + _dot(vsp_scr[bd, 1], onehot[d])
                      + _dot(vsp_scr[bd, 2], onehot[d]))
                for s in range(SCAN_GROUP):
                    vb_scr[bd, s] = vb[:, s * LANES:(s + 1) * LANES]

        base = (pl.multiple_of(t0, SCAN_GROUP), pl.multiple_of(tb - SCAN_GROUP - t0, SCAN_GROUP))
        for bi in range(nb):
            for d in range(2):
                bd = bi * 2 + d
                for p in range(npair):
                    c = bd * npair + p
                    cols = pl.ds(p * LANES, LANES)
                    r_w, kk_w, w_w, b_w, k_w = [ref[bi, pl.ds(base[d], SCAN_GROUP), cols] for ref in rows[d]]
                    st = s_scr[c]
                    acc_lo = acc_scr[2 * c]
                    acc_hi = acc_scr[2 * c + 1]
                    for s in range(SCAN_GROUP):
                        i = s if d == 0 else SCAN_GROUP - 1 - s
                        at_t = lane == base[d] + i
                        row = lambda win: win[i:i + 1, :]
                        lo_s, hi_s = seg_sums(st * row(kk_w))
                        s_kk = jnp.where(low, lo_s, hi_s)
                        v_col = jnp.where(low,
                                          vb_scr[bd, s, pl.ds(2 * p * RWKV_HEAD, RWKV_HEAD), :],
                                          vb_scr[bd, s, pl.ds((2 * p + 1) * RWKV_HEAD, RWKV_HEAD), :])
                        st = st * row(w_w) - s_kk * row(b_w) + v_col * row(k_w)
                        y_lo, y_hi = seg_sums(st * row(r_w))
                        acc_lo = jnp.where(at_t, y_lo, acc_lo)
                        acc_hi = jnp.where(at_t, y_hi, acc_hi)
                    s_scr[c] = st
                    acc_scr[2 * c] = acc_lo
                    acc_scr[2 * c + 1] = acc_hi
        return carry

    lax.fori_loop(0, tb // SCAN_GROUP, group, 0)

    for bi in range(nb):
        for d in range(2):
            for p in range(npair):
                c = (bi * 2 + d) * npair + p
                both = jnp.concatenate([acc_scr[2 * c], acc_scr[2 * c + 1]], axis=0)
                outs[d][bi, :, p * LANES:(p + 1) * LANES] = both.T


def rwkv_scan(r_src, kk, dec, bb, kd, vt):
    nb_total, s, c = kk.shape
    nb, tb = SCAN_NB, SCAN_TB
    nt = s // tb
    nct = CTX_LEN // tb

    def bwd(j):
        return jnp.where(j < nct, nct - 1 - j, nt - 1 - (j - nct))

    tok_f = lambda col: pl.BlockSpec((nb, tb, c), lambda g, j: (g, j, col))
    tok_b = lambda col: pl.BlockSpec((nb, tb, c), lambda g, j: (g, bwd(j), col))
    dir_f = pl.BlockSpec((None, nb, tb, c), lambda g, j: (0, g, j, 0))
    dir_b = pl.BlockSpec((None, nb, tb, c), lambda g, j: (1, g, bwd(j), 0))
    n_chain = nb * 2 * (RWKV_HEADS // 2)
    y_shape = jax.ShapeDtypeStruct((nb_total, s, c), F32)
    return pl.pallas_call(
        _scan_kernel,
        grid=(nb_total // nb, nt),
        in_specs=[tok_f(0), tok_f(0), dir_f, dir_f, dir_f,
                  pl.BlockSpec((nb, c, tb), lambda g, j: (g, 0, j)),
                  tok_b(0), tok_b(0), dir_b, dir_b, dir_b,
                  pl.BlockSpec((nb, c, tb), lambda g, j: (g, 0, bwd(j)))],
        out_specs=[pl.BlockSpec((nb, tb, c), lambda g, j: (g, j, 0)),
                   pl.BlockSpec((nb, tb, c), lambda g, j: (g, bwd(j), 0))],
        out_shape=[y_shape, y_shape],
        scratch_shapes=[pltpu.VMEM((n_chain, RWKV_HEAD, LANES), F32),
                        pltpu.VMEM((2 * n_chain, RWKV_HEAD, LANES), F32),
                        pltpu.VMEM((nb * 2, 3, c, tb), BF16),
                        pltpu.VMEM((nb * 2, SCAN_GROUP, c, LANES), F32)],
        compiler_params=_cp(("parallel", "arbitrary")),
        name="rwkv_scan",
    )(r_src, kk, dec, bb, kd, vt, r_src, kk, dec, bb, kd, vt)


def _readout_kernel(yf_ref, yb_ref, r_ref, k_ref, v_ref, gd_ref,
                    cb_ref, cc_ref, cx_ref, pc_ref, px_ref, nc_ref, nx_ref,
                    gup_ref, rk_ref, lw_ref, lb_ref, cw_ref, o_ref):
    i = pl.program_id(1)
    n_tiles = pl.num_programs(1)
    ones = _head_ones(RWKV_WIDTH, RWKV_HEAD)
    y = yf_ref[0] + yb_ref[0]
    mean = _dot_hi(y, ones) * (1.0 / RWKV_HEAD)
    yc = y - mean
    var = _dot_hi(yc * yc, ones) * (1.0 / RWKV_HEAD)
    out = yc * lax.rsqrt(var + LNX_EPS) * lw_ref[...] + lb_ref[...]
    bonus = _dot_hi(r_ref[0] * k_ref[0] * rk_ref[...], ones) * v_ref[0]
    gate = _dot_hi(jax.nn.sigmoid(gd_ref[0]), gup_ref[...])
    o_ref[0, :, 0:RWKV_WIDTH] = ((out + bonus) * gate).astype(BF16)

    u = cc_ref[0] * cx_ref[0]
    tm = u.shape[0]
    row = lax.broadcasted_iota(jnp.int32, u.shape, 0)
    has_prev = jnp.logical_and(i != 0, i != CTX_LEN // tm)
    has_next = jnp.logical_and(i != CTX_LEN // tm - 1, i != n_tiles - 1)
    u_prev_edge = jnp.where(has_prev, pc_ref[0, 7:8, :] * px_ref[0, 7:8, :], 0.0)
    u_next_edge = jnp.where(has_next, nc_ref[0, 0:1, :] * nx_ref[0, 0:1, :], 0.0)
    u_prev = jnp.where(row == 0, u_prev_edge, pltpu.roll(u, 1, 0))
    u_next = jnp.where(row == tm - 1, u_next_edge, pltpu.roll(u, tm - 1, 0))
    conv = u_prev * cw_ref[0:1, :] + u * cw_ref[1:2, :] + u_next * cw_ref[2:3, :]
    o_ref[0, :, RWKV_WIDTH:] = (cb_ref[0] * conv).astype(BF16)


def rwkv_readout(yf, yb, rkv, gd, conv, g_up, r_k, lnx_w, lnx_b, conv_w):
    nb, s, c = yf.shape
    tm = TOKEN_TILE
    n_tiles = s // tm
    per8 = tm // 8
    last8 = s // 8 - 1
    tok = lambda col, n=c: pl.BlockSpec((1, tm, n), lambda b, i: (b, i, col))
    prev = lambda col: pl.BlockSpec((1, 8, c), lambda b, i: (b, jnp.maximum(i * per8 - 1, 0), col))
    nxt = lambda col: pl.BlockSpec((1, 8, c), lambda b, i: (b, jnp.minimum((i + 1) * per8, last8), col))
    full = lambda shape: pl.BlockSpec(shape, lambda b, i: (0,) * len(shape))
    return pl.pallas_call(
        _readout_kernel,
        grid=(nb, n_tiles),
        in_specs=[tok(0), tok(0), tok(0), tok(1), tok(2), tok(0, 128),
                  tok(0), tok(1), tok(2), prev(1), prev(2), nxt(1), nxt(2),
                  full((GATE_LORA, c)), full((1, c)), full((1, c)), full((1, c)), full((3, c))],
        out_specs=pl.BlockSpec((1, tm, 2 * c), lambda b, i: (b, i, 0)),
        out_shape=jax.ShapeDtypeStruct((nb, s, 2 * c), BF16),
        compiler_params=_cp(("parallel", "parallel")),
        name="rwkv_readout",
    )(yf, yb, rkv, rkv, rkv, gd, conv, conv, conv, conv, conv, conv, conv,
      g_up, r_k.reshape(1, c), lnx_w.reshape(1, c), lnx_b.reshape(1, c), conv_w)


def _softmax_rows(s):
    m = jnp.max(s, axis=-1, keepdims=True)
    e = jnp.exp(s - m)
    return e / jnp.sum(e, axis=-1, keepdims=True)


def _diff_kernel(q_ref, k_ref, v_ref, lam_ref, sub_ref, o_ref, *, lam_init):
    i = pl.program_id(2)
    lam = lam_ref[...]
    lam_full = (jnp.exp(jnp.sum(lam[0:1] * lam[1:2], axis=1, keepdims=True))
                - jnp.exp(jnp.sum(lam[2:3] * lam[3:4], axis=1, keepdims=True)) + lam_init)
    q = q_ref[0]
    lane = lax.broadcasted_iota(jnp.int32, q.shape, 1)
    q1 = jnp.where(lane < DIFF_HEAD, q, jnp.zeros_like(q))
    q2 = jnp.where(lane < DIFF_HEAD, jnp.zeros_like(q), q)
    scale = DIFF_HEAD ** -0.5

    def attend(n_keys):
        k = k_ref[0, 0:n_keys, :]
        v = v_ref[0, 0:n_keys, :]
        p1 = _softmax_rows(_dot_nt(q1, k) * scale)
        p2 = _softmax_rows(_dot_nt(q2, k) * scale)
        o = _dot((p1 - lam_full * p2).astype(BF16), v)
        ms = jnp.mean(o * o, axis=-1, keepdims=True)
        o = o * lax.rsqrt(ms + SUBLN_EPS) * sub_ref[...] * (1.0 - lam_init)
        o_ref[0] = o.astype(BF16)

    @pl.when(i == 0)
    def _():
        attend(CTX_LEN)

    @pl.when(i != 0)
    def _():
        attend(k_ref.shape[1])


def diff_attention(dq, dk, dv, lam, subln, lam_init):
    nb, s, _ = dq.shape
    tm = TOKEN_TILE
    return pl.pallas_call(
        functools.partial(_diff_kernel, lam_init=lam_init),
        grid=(nb, DIFF_HEADS, s // tm),
        in_specs=[pl.BlockSpec((1, tm, LANES), lambda b, h, i: (b, i, h)),
                  pl.BlockSpec((1, s, LANES), lambda b, h, i: (b, 0, h)),
                  pl.BlockSpec((1, s, LANES), lambda b, h, i: (b, 0, h)),
                  pl.BlockSpec((4, DIFF_HEAD), lambda b, h, i: (0, 0)),
                  pl.BlockSpec((1, DIFF_VHEAD), lambda b, h, i: (0, 0))],
        out_specs=pl.BlockSpec((1, tm, LANES), lambda b, h, i: (b, i, h)),
        out_shape=jax.ShapeDtypeStruct((nb, s, DIFF_WIDTH), BF16),
        compiler_params=_cp(("parallel", "parallel", "parallel")),
        name="diff_attention",
    )(dq, dk, dv, lam, subln.reshape(1, DIFF_VHEAD))


def na_bias_table(rel_bias, rows):
    kr = min(NA_MAX_ROWS, rows)
    cols = jnp.arange(GRID_W)
    col_start = jnp.clip(cols - NA_COLS // 2, 0, GRID_W - NA_COLS)
    kc = jnp.arange(GRID_W)
    inside = (kc[None, :] >= col_start[:, None]) & (kc[None, :] < col_start[:, None] + NA_COLS)
    col_off = jnp.clip(kc[None, :] - cols[:, None] + (NA_COLS - 1), 0, 2 * NA_COLS - 2)
    place = jnp.arange(NA_MAX_ROWS)
    row_off = jnp.clip(jnp.arange(kr)[None, :] - place[:, None] + (NA_MAX_ROWS - 1), 0, 2 * NA_MAX_ROWS - 2)
    b = rel_bias[:, row_off[:, None, :, None], col_off[None, :, None, :]]
    b = jnp.where(inside[None, None, :, None, :], b, NEG_BIG)
    return b.reshape(rel_bias.shape[0], NA_MAX_ROWS, GRID_W, kr * GRID_W).astype(F32)


def _na_kernel(q_ref, k_ref, v_ref, bias_ref, o_ref, *, rows):
    kr = min(NA_MAX_ROWS, rows)
    scale = NA_HEAD ** -0.5
    w = GRID_W
    lane_q = lax.broadcasted_iota(jnp.int32, (2 * w, LANES), 1)
    row_q = lax.broadcasted_iota(jnp.int32, (2 * w, LANES), 0)
    own = (lane_q < NA_HEAD) == (row_q < w)
    first_head = lax.broadcasted_iota(jnp.int32, (w, LANES), 1) < NA_HEAD
    k_ctx = k_ref[0, 0:CTX_LEN, :]
    v_ctx = v_ref[0, 0:CTX_LEN, :]

    def two_heads(q):
        q2 = jnp.concatenate([q, q], axis=0)
        return jnp.where(own, q2, jnp.zeros_like(q2))

    def merge(o):
        return jnp.where(first_head, o[0:w], o[w:2 * w])

    def body(r, carry):
        r_start = jnp.clip(r - kr // 2, 0, rows - kr)
        q = two_heads(q_ref[0, pl.ds(pl.multiple_of(CTX_LEN + r * w, w), w), :])
        base = pl.multiple_of(CTX_LEN + r_start * w, w)
        k_win = k_ref[0, pl.ds(base, kr * w), :]
        v_win = v_ref[0, pl.ds(base, kr * w), :]
        place = r - r_start
        bias = jnp.concatenate([bias_ref[0, place], bias_ref[1, place]], axis=0)
        s_loc = _dot_nt(q, k_win) * scale + bias
        s_ctx = _dot_nt(q, k_ctx) * scale
        m = jnp.maximum(jnp.max(s_loc, axis=-1, keepdims=True), jnp.max(s_ctx, axis=-1, keepdims=True))
        e_loc = jnp.exp(s_loc - m)
        e_ctx = jnp.exp(s_ctx - m)
        z = jnp.sum(e_loc, axis=-1, keepdims=True) + jnp.sum(e_ctx, axis=-1, keepdims=True)
        o = (_dot((e_ctx / z).astype(BF16), v_ctx) + _dot((e_loc / z).astype(BF16), v_win))
        o_ref[0, pl.ds(pl.multiple_of(CTX_LEN + r * w, w), w), :] = merge(o).astype(BF16)
        return carry

    lax.fori_loop(0, rows, body, 0)

    for blk in range(CTX_LEN // w):
        q = two_heads(q_ref[0, blk * w:(blk + 1) * w, :])
        p = _softmax_rows(_dot_nt(q, k_ctx) * scale)
        o_ref[0, blk * w:(blk + 1) * w, :] = merge(_dot(p.astype(BF16), v_ctx)).astype(BF16)


def na_attention(nq, nk, nv, bias_tab, rows):
    nb, s, _ = nq.shape
    kr = min(NA_MAX_ROWS, rows)
    seq = pl.BlockSpec((1, s, LANES), lambda b, p: (b, 0, p))
    return pl.pallas_call(
        functools.partial(_na_kernel, rows=rows),
        grid=(nb, NA_HEADS // 2),
        in_specs=[seq, seq, seq,
                  pl.BlockSpec((2, NA_MAX_ROWS, GRID_W, kr * GRID_W), lambda b, p: (p, 0, 0, 0))],
        out_specs=seq,
        out_shape=jax.ShapeDtypeStruct((nb, s, NA_WIDTH), BF16),
        compiler_params=_cp(("parallel", "parallel")),
        name="na_attention",
    )(nq, nk, nv, bias_tab)


def _route(logits):
    lane = lax.broadcasted_iota(jnp.int32, logits.shape, 1)
    big = jnp.int32(LANES)
    is_g = jnp.logical_and(lane >= N_EXPERTS, lane < N_EXPERTS + N_GROUPS)
    lg = jnp.where(is_g, logits, NEG_BIG)
    mg = jnp.max(lg, axis=-1, keepdims=True)
    zg = jnp.sum(jnp.where(is_g, jnp.exp(lg - mg), 0.0), axis=-1, keepdims=True)
    pg_top = 1.0 / zg
    g_sel = jnp.min(jnp.where(jnp.logical_and(is_g, lg == mg), lane, big), axis=-1, keepdims=True) - N_EXPERTS
    in_g = jnp.logical_and(lane < N_EXPERTS, (lane // EXPERTS_PER_GROUP) == g_sel)
    le = jnp.where(in_g, logits, NEG_BIG)
    m1 = jnp.max(le, axis=-1, keepdims=True)
    i1 = jnp.min(jnp.where(jnp.logical_and(in_g, le == m1), lane, big), axis=-1, keepdims=True)
    rest = jnp.logical_and(in_g, lane != i1)
    le2 = jnp.where(rest, logits, NEG_BIG)
    m2 = jnp.max(le2, axis=-1, keepdims=True)
    i2 = jnp.min(jnp.where(jnp.logical_and(rest, le2 == m2), lane, big), axis=-1, keepdims=True)
    e2 = jnp.exp(m2 - m1)
    w1 = pg_top / (1.0 + e2)
    w2 = pg_top * e2 / (1.0 + e2)
    return jnp.where(lane == i1, w1, 0.0) + jnp.where(lane == i2, w2, 0.0)


def _outproj_kernel(x_ref, oa_ref, ob_ref, mod_ref, g_ref, w_ref, wr_ref, br_ref,
                    xo_ref, h_ref, comb_ref):
    m = mod_ref[0]
    half = oa_ref.shape[2]
    y = _dot(oa_ref[0], w_ref[0:half, :]) + _dot(ob_ref[0], w_ref[half:, :])
    x = x_ref[0] + m[2:3] * y
    xo_ref[0] = x
    h = _norm_mod(x, g_ref[...], m[3:4], m[4:5])
    h_ref[0] = h.astype(BF16)
    comb_ref[0] = _route(_dot_hi(h, wr_ref[...]) + br_ref[...])


def outproj_route(x, oa, ob, ob_col, modl, g, w_out, w_route, b_route):
    nb, s, d = x.shape
    tm = TOKEN_TILE
    half = d // 2
    tok = lambda n, col=0: pl.BlockSpec((1, tm, n), lambda b, i: (b, i, col))
    full = lambda shape: pl.BlockSpec(shape, lambda b, i: (0,) * len(shape))
    return pl.pallas_call(
        _outproj_kernel,
        grid=(nb, s // tm),
        in_specs=[tok(d), tok(half), tok(half, ob_col),
                  pl.BlockSpec((1, 8, d), _mod_index(nb)),
                  full((1, d)), full((d, d)), full((d, LANES)), full((1, LANES))],
        out_specs=[tok(d), tok(d), tok(LANES)],
        out_shape=[jax.ShapeDtypeStruct((nb, s, d), F32),
                   jax.ShapeDtypeStruct((nb, s, d), BF16),
                   jax.ShapeDtypeStruct((nb, s, LANES), F32)],
        compiler_params=_cp(("parallel", "parallel")),
        name="outproj_route",
    )(x, oa, ob, modl, g, w_out, w_route, b_route)


def _moe_kernel(h_ref, comb_ref, x_ref, mod_ref, w1_ref, w3_ref, w2_ref, o_ref, acc_ref):
    e = pl.program_id(2)

    @pl.when(e == 0)
    def _():
        acc_ref[...] = jnp.zeros_like(acc_ref)

    h = h_ref[0]
    a = _dot(h, w1_ref[0])
    a = a * jax.nn.sigmoid(a) * _dot(h, w3_ref[0])
    y = _dot(a.astype(BF16), w2_ref[0])
    comb = comb_ref[0]
    lane = lax.broadcasted_iota(jnp.int32, comb.shape, 1)
    gate = jnp.sum(jnp.where(lane == e, comb, 0.0), axis=-1, keepdims=True)
    acc_ref[...] += gate * y

    @pl.when(e == pl.num_programs(2) - 1)
    def _():
        o_ref[0] = x_ref[0] + mod_ref[0][5:6] * acc_ref[...]


def moe_dense(h, comb, x, modl, w1, w3, w2):
    nb, s, d = x.shape
    tm = TOKEN_TILE
    n_exp = w1.shape[0]
    tok = lambda n: pl.BlockSpec((1, tm, n), lambda b, i, e: (b, i, 0))
    mod_idx = _mod_index(nb)
    return pl.pallas_call(
        _moe_kernel,
        grid=(nb, s // tm, n_exp),
        in_specs=[tok(d), tok(LANES), tok(d),
                  pl.BlockSpec((1, 8, d), lambda b, i, e: mod_idx(b, i)),
                  pl.BlockSpec((1, d, D_EXPERT), lambda b, i, e: (e, 0, 0)),
                  pl.BlockSpec((1, d, D_EXPERT), lambda b, i, e: (e, 0, 0)),
                  pl.BlockSpec((1, D_EXPERT, d), lambda b, i, e: (e, 0, 0))],
        out_specs=tok(d),
        out_shape=jax.ShapeDtypeStruct((nb, s, d), F32),
        scratch_shapes=[pltpu.VMEM((tm, d), F32)],
        compiler_params=_cp(("parallel", "parallel", "arbitrary")),
        name="moe_dense",
    )(h, comb, x, modl, w1, w3, w2)


def _final_kernel(x_ref, g_ref, o_ref):
    x = x_ref[0]
    ms = jnp.mean(x * x, axis=-1, keepdims=True)
    o_ref[0] = x * lax.rsqrt(ms + RMS_EPS) * g_ref[...]


def final_norm(x, g):
    nb, s, d = x.shape
    tm = TOKEN_TILE
    skip = CTX_LEN // tm
    return pl.pallas_call(
        _final_kernel,
        grid=(nb, (s - CTX_LEN) // tm),
        in_specs=[pl.BlockSpec((1, tm, d), lambda b, i: (b, i + skip, 0)),
                  pl.BlockSpec((1, d), lambda b, i: (0, 0))],
        out_specs=pl.BlockSpec((1, tm, d), lambda b, i: (b, i, 0)),
        out_shape=jax.ShapeDtypeStruct((nb, s - CTX_LEN, d), F32),
        compiler_params=_cp(("parallel", "parallel")),
        name="final_norm",
    )(x, g.reshape(1, d))


def kernel(x, c, ctx, c_ctx, ada_w, ada_b, norm_g, final_g, ev_w_in, ev_w_out, ev_decay_w0, ev_decay_up, ev_iclr_a0, ev_iclr_up, ev_gate_up, ev_k_k, ev_k_a, ev_r_k, ev_lnx_w, ev_lnx_b, ev_conv_w, od_w_in, od_w_out, od_lambda, od_subln, od_rel_bias, moe_wg, moe_bg, moe_we, moe_be, moe_w1, moe_w3, moe_w2):
    nb, seq, d = x.shape
    rows = seq // GRID_W
    depth = ada_w.shape[0]
    assert nb <= 15 and nb % SCAN_NB == 0 and seq % TOKEN_TILE == 0 and ctx.shape[1] == CTX_LEN

    xs = jnp.concatenate([ctx, x], axis=1)
    cond = jnp.zeros((16, d), F32).at[:nb].set(c).at[nb].set(c_ctx)
    mod = ada_all(cond, ada_w, ada_b)
    mod = mod[:, :nb + 1].reshape(depth, nb + 1, 6, d)
    mod = jnp.concatenate([mod, jnp.zeros((depth, nb + 1, 2, d), F32)], axis=2)
    cos_t, sin_t = rope_tables(seq)

    for l in range(depth):
        i = l // 2
        modl = mod[l]
        g1 = norm_g[l, 0].reshape(1, d)
        g2 = norm_g[l, 1].reshape(1, d)
        if l % 2 == 0:
            w_in = ev_w_in[i].astype(BF16)
            wvt = ev_w_in[i][:, 2 * RWKV_WIDTH:3 * RWKV_WIDTH].T.astype(BF16)
            rkv, lora, gd, conv, vt = inproj_even(xs, modl, g1, w_in, wvt)
            kk, dec, bb, kd = rwkv_prep(rkv, lora, ev_decay_w0[i], ev_decay_up[i], ev_iclr_a0[i],
                                        ev_iclr_up[i], ev_k_k[i], ev_k_a[i])
            yf, yb = rwkv_scan(rkv, kk, dec, bb, kd, vt)
            o = rwkv_readout(yf, yb, rkv, gd, conv, ev_gate_up[i], ev_r_k[i], ev_lnx_w[i],
                             ev_lnx_b[i], ev_conv_w[i])
            oa, ob, ob_col = o, o, 1
            w_out = ev_w_out[i]
        else:
            dq, dk, dv, nq, nk, nv = inproj_odd(xs, modl, g1, od_w_in[i].astype(BF16), cos_t, sin_t)
            lam_init = 0.8 - 0.6 * math.exp(-0.3 * l)
            oa = diff_attention(dq, dk, dv, od_lambda[i], od_subln[i], lam_init)
            ob = na_attention(nq, nk, nv, na_bias_table(od_rel_bias[i], rows), rows)
            ob_col = 0
            w_out = od_w_out[i]
        w_route = jnp.concatenate(
            [moe_we[l], moe_wg[l], jnp.zeros((d, LANES - N_EXPERTS - N_GROUPS), F32)], axis=1)
        b_route = jnp.concatenate(
            [moe_be[l], moe_bg[l], jnp.zeros((LANES - N_EXPERTS - N_GROUPS,), F32)]).reshape(1, LANES)
        xs, h2, comb = outproj_route(xs, oa, ob, ob_col, modl, g2, w_out.astype(BF16), w_route, b_route)
        xs = moe_dense(h2, comb, xs, modl, moe_w1[l].astype(BF16), moe_w3[l].astype(BF16),
                       moe_w2[l].astype(BF16))
    return final_norm(xs, final_g)
```

```python
import functools
import math

import jax
import jax.numpy as jnp
from jax import lax
from jax.experimental import pallas as pl
from jax.experimental.pallas import tpu as pltpu

F32 = jnp.float32
BF16 = jnp.bfloat16
HI = lax.Precision.HIGHEST

D_MODEL = 1024
DEPTH = 4
GRID_W = 64
CTX_LEN = 256

RWKV_WIDTH = 512
RWKV_HEAD = 64
RWKV_HEADS = 8
DECAY_LORA = 64
ICLR_LORA = 64
GATE_LORA = 128
CONV_WIDTH = 512
LNX_EPS = 64e-5

DIFF_HEADS = 4
DIFF_HEAD = 64
DIFF_VHEAD = 128
DIFF_WIDTH = 512
NA_HEADS = 8
NA_HEAD = 64
NA_WIDTH = 512
NA_MAX_ROWS = 8
NA_COLS = 16
ROPE_BASE = 10000.0
SUBLN_EPS = 1e-5

N_GROUPS = 4
EXPERTS_PER_GROUP = 4
N_EXPERTS = 16
D_EXPERT = 512
RMS_EPS = 1e-6

EVEN_PROJ = 3456
ODD_PROJ = 3072

LANES = 128
TOKEN_TILE = 256
SCAN_TB = 128
SCAN_GROUP = 16
SCAN_VECS = 5
SCAN_PAIRS = RWKV_HEADS // 2
SCAN_ROWS = SCAN_VECS * SCAN_PAIRS * RWKV_HEAD
SCAN_STEP_ROWS = (SCAN_VECS - 1) * SCAN_PAIRS * RWKV_HEAD
SCAN_K = 3 * 2 * SCAN_TB
SCAN_NG = SCAN_TB // SCAN_GROUP
VMEM_LIMIT = 56 * 1024 * 1024
NEG_BIG = -1e30


def _cp(sem, vmem=VMEM_LIMIT):
    return pltpu.CompilerParams(dimension_semantics=sem, vmem_limit_bytes=vmem)


def _dot(a, b):
    return jnp.dot(a, b, preferred_element_type=F32)


def _dot_hi(a, b):
    return jnp.dot(a, b, preferred_element_type=F32, precision=HI)


def _dot_nt(a, b):
    return lax.dot_general(a, b, (((1,), (1,)), ((), ())), preferred_element_type=F32)


def _head_ones(width, head):
    i = lax.broadcasted_iota(jnp.int32, (width, width), 0) // head
    j = lax.broadcasted_iota(jnp.int32, (width, width), 1) // head
    return (i == j).astype(F32)


def _ada_kernel(cond_ref, w_ref, b_ref, o_ref):
    cond = cond_ref[...]
    s = cond * jax.nn.sigmoid(cond)
    o_ref[0] = _dot_hi(s, w_ref[0]) + b_ref[0]


def ada_all(cond, ada_w, ada_b):
    depth, d, n = ada_w.shape
    tn = 1536
    return pl.pallas_call(
        _ada_kernel,
        grid=(depth, n // tn),
        in_specs=[pl.BlockSpec((16, d), lambda l, j: (0, 0)),
                  pl.BlockSpec((1, d, tn), lambda l, j: (l, 0, j)),
                  pl.BlockSpec((1, 1, tn), lambda l, j: (l, 0, j))],
        out_specs=pl.BlockSpec((1, 16, tn), lambda l, j: (l, 0, j)),
        out_shape=jax.ShapeDtypeStruct((depth, 16, n), F32),
        compiler_params=_cp(("parallel", "parallel")),
        name="ada_mod",
    )(cond, ada_w, ada_b.reshape(depth, 1, n))


def _mod_index(n_batch):
    return lambda b, i: (jnp.where(i == 0, n_batch, b), 0, 0)


def _norm_mod(x, g, shift, scale):
    ms = jnp.mean(x * x, axis=-1, keepdims=True)
    return x * lax.rsqrt(ms + RMS_EPS) * g * (1.0 + scale) + shift


def _inproj_even_kernel(x_ref, mod_ref, g_ref, w_ref, rkv_ref, lora_ref, gd_ref, conv_ref):
    m = mod_ref[0]
    h = _norm_mod(x_ref[0], g_ref[...], m[0:1], m[1:2]).astype(BF16)
    p = _dot(h, w_ref[...])
    rkv_ref[0] = p[:, 0:1536]
    lora_ref[0] = p[:, 1536:1792]
    gd_ref[0] = p[:, 1792:1920]
    conv_ref[0] = p[:, 1920:3456]


def inproj_even(x, modl, g, w):
    nb, s, d = x.shape
    tm = TOKEN_TILE
    tok = lambda n: pl.BlockSpec((1, tm, n), lambda b, i: (b, i, 0))
    return pl.pallas_call(
        _inproj_even_kernel,
        grid=(nb, s // tm),
        in_specs=[tok(d),
                  pl.BlockSpec((1, 8, d), _mod_index(nb)),
                  pl.BlockSpec((1, d), lambda b, i: (0, 0)),
                  pl.BlockSpec((d, EVEN_PROJ), lambda b, i: (0, 0))],
        out_specs=[tok(1536), tok(256), tok(128), tok(1536)],
        out_shape=[jax.ShapeDtypeStruct((nb, s, 1536), F32),
                   jax.ShapeDtypeStruct((nb, s, 256), F32),
                   jax.ShapeDtypeStruct((nb, s, 128), F32),
                   jax.ShapeDtypeStruct((nb, s, 1536), F32)],
        compiler_params=_cp(("parallel", "parallel")),
        name="inproj_even",
    )(x, modl, g, w)


def _rope(x, cos, sin_signed):
    lane = lax.broadcasted_iota(jnp.int32, x.shape, 1)
    partner = jnp.where((lane % 32) < 16, pltpu.roll(x, LANES - 16, 1), pltpu.roll(x, 16, 1))
    return x * cos + partner * sin_signed


def _inproj_odd_kernel(x_ref, mod_ref, g_ref, w_ref, cos_ref, sin_ref,
                       dq_ref, dk_ref, dv_ref, nq_ref, nk_ref, nv_ref):
    m = mod_ref[0]
    h = _norm_mod(x_ref[0], g_ref[...], m[0:1], m[1:2]).astype(BF16)
    p = _dot(h, w_ref[...])
    cos = cos_ref[...]
    sin = sin_ref[...]
    for out_ref, base in ((dq_ref, 0), (dk_ref, 512)):
        for c in range(4):
            blk = p[:, base + c * LANES: base + (c + 1) * LANES]
            out_ref[0, :, c * LANES:(c + 1) * LANES] = _rope(blk, cos, sin).astype(BF16)
    dv_ref[0] = p[:, 1024:1536].astype(BF16)
    nq_ref[0] = p[:, 1536:2048].astype(BF16)
    nk_ref[0] = p[:, 2048:2560].astype(BF16)
    nv_ref[0] = p[:, 2560:3072].astype(BF16)


def inproj_odd(x, modl, g, w, cos_t, sin_t):
    nb, s, d = x.shape
    tm = TOKEN_TILE
    tok = lambda n: pl.BlockSpec((1, tm, n), lambda b, i: (b, i, 0))
    tab = pl.BlockSpec((tm, LANES), lambda b, i: (i, 0))
    return pl.pallas_call(
        _inproj_odd_kernel,
        grid=(nb, s // tm),
        in_specs=[tok(d),
                  pl.BlockSpec((1, 8, d), _mod_index(nb)),
                  pl.BlockSpec((1, d), lambda b, i: (0, 0)),
                  pl.BlockSpec((d, ODD_PROJ), lambda b, i: (0, 0)),
                  tab, tab],
        out_specs=[tok(512)] * 6,
        out_shape=[jax.ShapeDtypeStruct((nb, s, 512), BF16)] * 6,
        compiler_params=_cp(("parallel", "parallel")),
        name="inproj_odd",
    )(x, modl, g, w, cos_t, sin_t)


def rope_tables(seq):
    n_freq = DIFF_HEAD // 4
    inv_freq = ROPE_BASE ** (-jnp.arange(n_freq, dtype=F32) / n_freq)
    t = jnp.arange(seq, dtype=jnp.int32)
    pos = jnp.stack([t // GRID_W, t % GRID_W], axis=-1).astype(F32)
    ang = pos[:, :, None] * inv_freq
    cos, sin = jnp.cos(ang), jnp.sin(ang)
    cos64 = jnp.concatenate([cos[:, 0], cos[:, 0], cos[:, 1], cos[:, 1]], axis=-1)
    sin64 = jnp.concatenate([-sin[:, 0], sin[:, 0], -sin[:, 1], sin[:, 1]], axis=-1)
    cos_t = jnp.concatenate([jnp.ones((CTX_LEN, 64), F32), cos64], axis=0)
    sin_t = jnp.concatenate([jnp.zeros((CTX_LEN, 64), F32), sin64], axis=0)
    return jnp.tile(cos_t, (1, 2)), jnp.tile(sin_t, (1, 2))


def _split3(x):
    hi = x.astype(BF16)
    r1 = x - hi.astype(F32)
    mid = r1.astype(BF16)
    lo = (r1 - mid.astype(F32)).astype(BF16)
    return hi, mid, lo


def _prep_kernel(r_ref, k_ref, lora_ref, w0_ref, wup_ref, a0_ref, aup_ref, kk_w_ref, ka_ref, xt_ref):
    k = k_ref[0]
    r = r_ref[0]
    lora = lora_ref[0]
    tb = SCAN_TB
    ones = _head_ones(RWKV_WIDTH, RWKV_HEAD)
    kk = k * kk_w_ref[...]
    ss = _dot_hi(kk * kk, ones)
    kk = kk * lax.rsqrt(jnp.maximum(ss, 1e-24))
    ti = lax.broadcasted_iota(jnp.int32, (tb, tb), 0)
    tj = lax.broadcasted_iota(jnp.int32, (tb, tb), 1)
    same_group = (ti // SCAN_GROUP) == (tj // SCAN_GROUP)
    upto = (jnp.logical_and(same_group, tj <= ti).astype(F32),
            jnp.logical_and(same_group, tj >= ti).astype(F32))

    def put(d, vec, x):
        for term_i, term in enumerate(_split3(x.T)):
            for h in range(RWKV_HEADS):
                row0 = (vec * SCAN_PAIRS + h // 2) * RWKV_HEAD
                lane0 = (term_i * 2 + h % 2) * SCAN_TB
                xt_ref[0, d, 0, row0:row0 + RWKV_HEAD, lane0:lane0 + SCAN_TB] = (
                    term[h * RWKV_HEAD:(h + 1) * RWKV_HEAD, :])

    for d in range(2):
        wd = lora[:, d * 64:(d + 1) * 64]
        ad = lora[:, 128 + d * 64:128 + (d + 1) * 64]
        w = w0_ref[d:d + 1, :] + _dot_hi(jnp.tanh(wd), wup_ref[d])
        log_decay = -math.exp(-0.5) * jax.nn.sigmoid(w)
        a = jax.nn.sigmoid(a0_ref[d:d + 1, :] + _dot_hi(ad, aup_ref[d]))
        log_g = _dot_hi(upto[d], log_decay)
        g = jnp.exp(log_g)
        g_inv = jnp.exp(-log_g)
        put(d, 0, kk * jnp.exp(log_g - log_decay))
        put(d, 1, kk * a * g_inv)
        put(d, 2, k * (1.0 + (a - 1.0) * ka_ref[...]) * g_inv)
        put(d, 3, r * g)
        put(d, 4, g)


def rwkv_prep(rkv, lora, w0, wup, a0, aup, k_k, k_a):
    nb, s, _ = rkv.shape
    tb = SCAN_TB
    c = RWKV_WIDTH
    full = lambda shape: pl.BlockSpec(shape, lambda b, i: (0,) * len(shape))
    return pl.pallas_call(
        _prep_kernel,
        grid=(nb, s // tb),
        in_specs=[pl.BlockSpec((1, tb, c), lambda b, i: (b, i, 0)),
                  pl.BlockSpec((1, tb, c), lambda b, i: (b, i, 1)),
                  pl.BlockSpec((1, tb, 256), lambda b, i: (b, i, 0)),
                  full((2, c)), full((2, DECAY_LORA, c)), full((2, c)), full((2, ICLR_LORA, c)),
                  full((1, c)), full((1, c))],
        out_specs=pl.BlockSpec((1, 2, 1, SCAN_ROWS, SCAN_K), lambda b, i: (b, 0, i, 0, 0)),
        out_shape=jax.ShapeDtypeStruct((nb, 2, s // tb, SCAN_ROWS, SCAN_K), BF16),
        compiler_params=_cp(("parallel", "parallel")),
        name="rwkv_prep",
    )(rkv, rkv, lora, w0, wup, a0, aup, k_k.reshape(1, c), k_a.reshape(1, c))


def scan_selectors():
    tb, grp = SCAN_TB, SCAN_GROUP
    src = jnp.arange(SCAN_K)
    term, head, time = src // (2 * tb), (src // tb) % 2, src % tb
    dst = jnp.arange(SCAN_NG * LANES)
    gather = ((time // grp)[:, None] == (dst // LANES)[None, :]) & (
        (term * 2 * grp + head * grp + time % grp)[:, None] == (dst % LANES)[None, :])
    j = jnp.arange(LANES)
    j_valid, j_head, j_step = j < 3 * 2 * grp, (j // grp) % 2, j % grp
    col = jnp.arange(grp * LANES)
    col_step, col_head = col // LANES, (col % LANES) // RWKV_HEAD
    base = j_valid[:, None] & (j_head[:, None] == col_head[None, :])
    expand = jnp.stack([base & (j_step[:, None] == col_step[None, :]),
                        base & (j_step[:, None] == grp - 1 - col_step[None, :])])
    lane_head = jnp.arange(LANES) // RWKV_HEAD
    base_l = j_valid[:, None] & (j_head[:, None] == lane_head[None, :])
    last = jnp.stack([base_l & (j_step[:, None] == grp - 1), base_l & (j_step[:, None] == 0)])
    return gather.astype(BF16), expand.astype(BF16), last.astype(BF16)


def _scan_kernel(xtf_ref, xtb_ref, vf_ref, vb_ref, gather_ref, expand_ref, last_ref, yf_ref, yb_ref,
                 t_scr, c_scr, big_scr, end_scr):
    tb, grp = SCAN_TB, SCAN_GROUP
    j = pl.program_id(1)

    @pl.when(j == 0)
    def _():
        t_scr[...] = jnp.zeros_like(t_scr)

    xts = (xtf_ref, xtb_ref)
    vs = (vf_ref, vb_ref)
    outs = (yf_ref, yb_ref)
    step_row = lax.broadcasted_iota(jnp.int32, (grp, LANES), 0)

    for d in range(2):
        packed = _dot(xts[d][0], gather_ref[...]).astype(BF16)
        for gi in range(SCAN_NG):
            c_scr[d, gi] = packed[:, gi * LANES:(gi + 1) * LANES]

    def group(g, carry):
        off = g * grp
        for d in range(2):
            lhs = c_scr[d, g if d == 0 else SCAN_NG - 1 - g]
            big_scr[d] = _dot(lhs[0:SCAN_STEP_ROWS], expand_ref[d])
            end_scr[d] = _dot(lhs[SCAN_STEP_ROWS:SCAN_ROWS], last_ref[d])
        for d in range(2):
            base = pl.multiple_of(off if d == 0 else tb - grp - off, grp)
            for p in range(SCAN_PAIRS):
                c = d * SCAN_PAIRS + p
                cols = pl.ds(p * LANES, LANES)
                v_win = vs[d][0, pl.ds(base, grp), cols]
                st = t_scr[c]
                y_win = jnp.zeros((grp, LANES), F32)
                for s in range(grp):
                    i = s if d == 0 else grp - 1 - s
                    col = lambda vec: big_scr[d, (vec * SCAN_PAIRS + p) * RWKV_HEAD:
                                              (vec * SCAN_PAIRS + p + 1) * RWKV_HEAD,
                                              s * LANES:(s + 1) * LANES]
                    s_kk = jnp.sum(st * col(0), axis=0, keepdims=True)
                    st = st - col(1) * s_kk + col(2) * v_win[i:i + 1, :]
                    y = jnp.sum(st * col(3), axis=0, keepdims=True)
                    y_win = jnp.where(step_row == i, y, y_win)
                t_scr[c] = st * end_scr[d, p * RWKV_HEAD:(p + 1) * RWKV_HEAD, :]
                outs[d][0, pl.ds(base, grp), cols] = y_win
        return carry

    lax.fori_loop(0, SCAN_NG, group, 0)


def rwkv_scan(xt, v_src):
    nb, _, nt, _, _ = xt.shape
    s = v_src.shape[1]
    c = RWKV_WIDTH
    tb = SCAN_TB
    nct = CTX_LEN // tb

    def bwd(j):
        return jnp.where(j < nct, nct - 1 - j, nt - 1 - (j - nct))

    gather, expand, last = scan_selectors()
    whole = pl.BlockSpec(memory_space=pltpu.VMEM)
    y_shape = jax.ShapeDtypeStruct((nb, s, c), F32)
    return pl.pallas_call(
        _scan_kernel,
        grid=(nb, nt),
        in_specs=[pl.BlockSpec((1, None, None, SCAN_ROWS, SCAN_K), lambda b, j: (b, 0, j, 0, 0)),
                  pl.BlockSpec((1, None, None, SCAN_ROWS, SCAN_K), lambda b, j: (b, 1, bwd(j), 0, 0)),
                  pl.BlockSpec((1, tb, c), lambda b, j: (b, j, 2)),
                  pl.BlockSpec((1, tb, c), lambda b, j: (b, bwd(j), 2)),
                  whole, whole, whole],
        out_specs=[pl.BlockSpec((1, tb, c), lambda b, j: (b, j, 0)),
                   pl.BlockSpec((1, tb, c), lambda b, j: (b, bwd(j), 0))],
        out_shape=[y_shape, y_shape],
        scratch_shapes=[pltpu.VMEM((2 * SCAN_PAIRS, RWKV_HEAD, LANES), F32),
                        pltpu.VMEM((2, SCAN_NG, SCAN_ROWS, LANES), BF16),
                        pltpu.VMEM((2, SCAN_STEP_ROWS, SCAN_GROUP * LANES), F32),
                        pltpu.VMEM((2, SCAN_PAIRS * RWKV_HEAD, LANES), F32)],
        compiler_params=_cp(("parallel", "arbitrary")),
        name="rwkv_scan",
    )(xt, xt, v_src, v_src, gather, expand, last)


def _readout_kernel(yf_ref, yb_ref, r_ref, k_ref, v_ref, gd_ref,
                    cb_ref, cc_ref, cx_ref, pc_ref, px_ref, nc_ref, nx_ref,
                    gup_ref, rk_ref, lw_ref, lb_ref, cw_ref, o_ref):
    i = pl.program_id(1)
    n_tiles = pl.num_programs(1)
    ones = _head_ones(RWKV_WIDTH, RWKV_HEAD)
    y = yf_ref[0] + yb_ref[0]
    mean = _dot_hi(y, ones) * (1.0 / RWKV_HEAD)
    yc = y - mean
    var = _dot_hi(yc * yc, ones) * (1.0 / RWKV_HEAD)
    out = yc * lax.rsqrt(var + LNX_EPS) * lw_ref[...] + lb_ref[...]
    bonus = _dot_hi(r_ref[0] * k_ref[0] * rk_ref[...], ones) * v_ref[0]
    gate = _dot_hi(jax.nn.sigmoid(gd_ref[0]), gup_ref[...])
    o_ref[0, :, 0:RWKV_WIDTH] = ((out + bonus) * gate).astype(BF16)

    u = cc_ref[0] * cx_ref[0]
    tm = u.shape[0]
    row = lax.broadcasted_iota(jnp.int32, u.shape, 0)
    has_prev = jnp.logical_and(i != 0, i != CTX_LEN // tm)
    has_next = jnp.logical_and(i != CTX_LEN // tm - 1, i != n_tiles - 1)
    u_prev_edge = jnp.where(has_prev, pc_ref[0, 7:8, :] * px_ref[0, 7:8, :], 0.0)
    u_next_edge = jnp.where(has_next, nc_ref[0, 0:1, :] * nx_ref[0, 0:1, :], 0.0)
    u_prev = jnp.where(row == 0, u_prev_edge, pltpu.roll(u, 1, 0))
    u_next = jnp.where(row == tm - 1, u_next_edge, pltpu.roll(u, tm - 1, 0))
    conv = u_prev * cw_ref[0:1, :] + u * cw_ref[1:2, :] + u_next * cw_ref[2:3, :]
    o_ref[0, :, RWKV_WIDTH:] = (cb_ref[0] * conv).astype(BF16)


def rwkv_readout(yf, yb, rkv, gd, conv, g_up, r_k, lnx_w, lnx_b, conv_w):
    nb, s, c = yf.shape
    tm = TOKEN_TILE
    n_tiles = s // tm
    per8 = tm // 8
    last8 = s // 8 - 1
    tok = lambda col, n=c: pl.BlockSpec((1, tm, n), lambda b, i: (b, i, col))
    prev = lambda col: pl.BlockSpec((1, 8, c), lambda b, i: (b, jnp.maximum(i * per8 - 1, 0), col))
    nxt = lambda col: pl.BlockSpec((1, 8, c), lambda b, i: (b, jnp.minimum((i + 1) * per8, last8), col))
    full = lambda shape: pl.BlockSpec(shape, lambda b, i: (0,) * len(shape))
    return pl.pallas_call(
        _readout_kernel,
        grid=(nb, n_tiles),
        in_specs=[tok(0), tok(0), tok(0), tok(1), tok(2), tok(0, 128),
                  tok(0), tok(1), tok(2), prev(1), prev(2), nxt(1), nxt(2),
                  full((GATE_LORA, c)), full((1, c)), full((1, c)), full((1, c)), full((3, c))],
        out_specs=pl.BlockSpec((1, tm, 2 * c), lambda b, i: (b, i, 0)),
        out_shape=jax.ShapeDtypeStruct((nb, s, 2 * c), BF16),
        compiler_params=_cp(("parallel", "parallel")),
        name="rwkv_readout",
    )(yf, yb, rkv, rkv, rkv, gd, conv, conv, conv, conv, conv, conv, conv,
      g_up, r_k.reshape(1, c), lnx_w.reshape(1, c), lnx_b.reshape(1, c), conv_w)


def _softmax_rows(s):
    m = jnp.max(s, axis=-1, keepdims=True)
    e = jnp.exp(s - m)
    return e / jnp.sum(e, axis=-1, keepdims=True)


def _diff_kernel(q_ref, k_ref, v_ref, lam_ref, sub_ref, o_ref, *, lam_init):
    i = pl.program_id(2)
    lam = lam_ref[...]
    lam_full = (jnp.exp(jnp.sum(lam[0:1] * lam[1:2], axis=1, keepdims=True))
                - jnp.exp(jnp.sum(lam[2:3] * lam[3:4], axis=1, keepdims=True)) + lam_init)
    q = q_ref[0]
    lane = lax.broadcasted_iota(jnp.int32, q.shape, 1)
    q1 = jnp.where(lane < DIFF_HEAD, q, jnp.zeros_like(q))
    q2 = jnp.where(lane < DIFF_HEAD, jnp.zeros_like(q), q)
    scale = DIFF_HEAD ** -0.5

    def attend(n_keys):
        k = k_ref[0, 0:n_keys, :]
        v = v_ref[0, 0:n_keys, :]
        p1 = _softmax_rows(_dot_nt(q1, k) * scale)
        p2 = _softmax_rows(_dot_nt(q2, k) * scale)
        o = _dot((p1 - lam_full * p2).astype(BF16), v)
        ms = jnp.mean(o * o, axis=-1, keepdims=True)
        o = o * lax.rsqrt(ms + SUBLN_EPS) * sub_ref[...] * (1.0 - lam_init)
        o_ref[0] = o.astype(BF16)

    @pl.when(i == 0)
    def _():
        attend(CTX_LEN)

    @pl.when(i != 0)
    def _():
        attend(k_ref.shape[1])


def diff_attention(dq, dk, dv, lam, subln, lam_init):
    nb, s, _ = dq.shape
    tm = TOKEN_TILE
    return pl.pallas_call(
        functools.partial(_diff_kernel, lam_init=lam_init),
        grid=(nb, DIFF_HEADS, s // tm),
        in_specs=[pl.BlockSpec((1, tm, LANES), lambda b, h, i: (b, i, h)),
                  pl.BlockSpec((1, s, LANES), lambda b, h, i: (b, 0, h)),
                  pl.BlockSpec((1, s, LANES), lambda b, h, i: (b, 0, h)),
                  pl.BlockSpec((4, DIFF_HEAD), lambda b, h, i: (0, 0)),
                  pl.BlockSpec((1, DIFF_VHEAD), lambda b, h, i: (0, 0))],
        out_specs=pl.BlockSpec((1, tm, LANES), lambda b, h, i: (b, i, h)),
        out_shape=jax.ShapeDtypeStruct((nb, s, DIFF_WIDTH), BF16),
        compiler_params=_cp(("parallel", "parallel", "parallel")),
        name="diff_attention",
    )(dq, dk, dv, lam, subln.reshape(1, DIFF_VHEAD))


def na_bias_table(rel_bias, rows):
    kr = min(NA_MAX_ROWS, rows)
    cols = jnp.arange(GRID_W)
    col_start = jnp.clip(cols - NA_COLS // 2, 0, GRID_W - NA_COLS)
    kc = jnp.arange(GRID_W)
    inside = (kc[None, :] >= col_start[:, None]) & (kc[None, :] < col_start[:, None] + NA_COLS)
    col_off = kc[None, :] - cols[:, None] + (NA_COLS - 1)
    row_off = jnp.arange(kr)[None, :] - jnp.arange(NA_MAX_ROWS)[:, None] + (NA_MAX_ROWS - 1)
    pick_row = (row_off[:, :, None] == jnp.arange(2 * NA_MAX_ROWS - 1)).astype(F32)
    pick_col = ((col_off[:, :, None] == jnp.arange(2 * NA_COLS - 1)) & inside[:, :, None]).astype(F32)
    b = jnp.einsum('hro,pjr,cko->hpcjk', rel_bias.astype(F32), pick_row, pick_col, precision=HI)
    b = jnp.where(inside[None, None, :, None, :], b, NEG_BIG)
    return b.reshape(rel_bias.shape[0], NA_MAX_ROWS, GRID_W, kr * GRID_W)


def _na_kernel(q_ref, k_ref, v_ref, bias_ref, o_ref, *, rows):
    kr = min(NA_MAX_ROWS, rows)
    scale = NA_HEAD ** -0.5
    w = GRID_W
    lane_q = lax.broadcasted_iota(jnp.int32, (2 * w, LANES), 1)
    row_q = lax.broadcasted_iota(jnp.int32, (2 * w, LANES), 0)
    own = (lane_q < NA_HEAD) == (row_q < w)
    first_head = lax.broadcasted_iota(jnp.int32, (w, LANES), 1) < NA_HEAD
    k_ctx = k_ref[0, 0:CTX_LEN, :]
    v_ctx = v_ref[0, 0:CTX_LEN, :]

    def two_heads(q):
        q2 = jnp.concatenate([q, q], axis=0)
        return jnp.where(own, q2, jnp.zeros_like(q2))

    def merge(o):
        return jnp.where(first_head, o[0:w], o[w:2 * w])

    def body(r, carry):
        r_start = jnp.clip(r - kr // 2, 0, rows - kr)
        q = two_heads(q_ref[0, pl.ds(pl.multiple_of(CTX_LEN + r * w, w), w), :])
        base = pl.multiple_of(CTX_LEN + r_start * w, w)
        k_win = k_ref[0, pl.ds(base, kr * w), :]
        v_win = v_ref[0, pl.ds(base, kr * w), :]
        place = r - r_start
        bias = jnp.concatenate([bias_ref[0, place], bias_ref[1, place]], axis=0)
        s_loc = _dot_nt(q, k_win) * scale + bias
        s_ctx = _dot_nt(q, k_ctx) * scale
        m = jnp.maximum(jnp.max(s_loc, axis=-1, keepdims=True), jnp.max(s_ctx, axis=-1, keepdims=True))
        e_loc = jnp.exp(s_loc - m)
        e_ctx = jnp.exp(s_ctx - m)
        z = jnp.sum(e_loc, axis=-1, keepdims=True) + jnp.sum(e_ctx, axis=-1, keepdims=True)
        o = (_dot((e_ctx / z).astype(BF16), v_ctx) + _dot((e_loc / z).astype(BF16), v_win))
        o_ref[0, pl.ds(pl.multiple_of(CTX_LEN + r * w, w), w), :] = merge(o).astype(BF16)
        return carry

    lax.fori_loop(0, rows, body, 0)

    for blk in range(CTX_LEN // w):
        q = two_heads(q_ref[0, blk * w:(blk + 1) * w, :])
        p = _softmax_rows(_dot_nt(q, k_ctx) * scale)
        o_ref[0, blk * w:(blk + 1) * w, :] = merge(_dot(p.astype(BF16), v_ctx)).astype(BF16)


def na_attention(nq, nk, nv, bias_tab, rows):
    nb, s, _ = nq.shape
    kr = min(NA_MAX_ROWS, rows)
    seq = pl.BlockSpec((1, s, LANES), lambda b, p: (b, 0, p))
    return pl.pallas_call(
        functools.partial(_na_kernel, rows=rows),
        grid=(nb, NA_HEADS // 2),
        in_specs=[seq, seq, seq,
                  pl.BlockSpec((2, NA_MAX_ROWS, GRID_W, kr * GRID_W), lambda b, p: (p, 0, 0, 0))],
        out_specs=seq,
        out_shape=jax.ShapeDtypeStruct((nb, s, NA_WIDTH), BF16),
        compiler_params=_cp(("parallel", "parallel")),
        name="na_attention",
    )(nq, nk, nv, bias_tab)


def _route(logits):
    lane = lax.broadcasted_iota(jnp.int32, logits.shape, 1)
    big = jnp.int32(LANES)
    is_g = jnp.logical_and(lane >= N_EXPERTS, lane < N_EXPERTS + N_GROUPS)
    lg = jnp.where(is_g, logits, NEG_BIG)
    mg = jnp.max(lg, axis=-1, keepdims=True)
    zg = jnp.sum(jnp.where(is_g, jnp.exp(lg - mg), 0.0), axis=-1, keepdims=True)
    pg_top = 1.0 / zg
    g_sel = jnp.min(jnp.where(jnp.logical_and(is_g, lg == mg), lane, big), axis=-1, keepdims=True) - N_EXPERTS
    in_g = jnp.logical_and(lane < N_EXPERTS, (lane // EXPERTS_PER_GROUP) == g_sel)
    le = jnp.where(in_g, logits, NEG_BIG)
    m1 = jnp.max(le, axis=-1, keepdims=True)
    i1 = jnp.min(jnp.where(jnp.logical_and(in_g, le == m1), lane, big), axis=-1, keepdims=True)
    rest = jnp.logical_and(in_g, lane != i1)
    le2 = jnp.where(rest, logits, NEG_BIG)
    m2 = jnp.max(le2, axis=-1, keepdims=True)
    i2 = jnp.min(jnp.where(jnp.logical_and(rest, le2 == m2), lane, big), axis=-1, keepdims=True)
    e2 = jnp.exp(m2 - m1)
    w1 = pg_top / (1.0 + e2)
    w2 = pg_top * e2 / (1.0 + e2)
    return jnp.where(lane == i1, w1, 0.0) + jnp.where(lane == i2, w2, 0.0)


def _outproj_kernel(x_ref, oa_ref, ob_ref, mod_ref, g_ref, w_ref, wr_ref, br_ref,
                    xo_ref, h_ref, comb_ref):
    m = mod_ref[0]
    half = oa_ref.shape[2]
    y = _dot(oa_ref[0], w_ref[0:half, :]) + _dot(ob_ref[0], w_ref[half:, :])
    x = x_ref[0] + m[2:3] * y
    xo_ref[0] = x
    h = _norm_mod(x, g_ref[...], m[3:4], m[4:5])
    h_ref[0] = h.astype(BF16)
    comb_ref[0] = _route(_dot_hi(h, wr_ref[...]) + br_ref[...])


def outproj_route(x, oa, ob, ob_col, modl, g, w_out, w_route, b_route):
    nb, s, d = x.shape
    tm = TOKEN_TILE
    half = d // 2
    tok = lambda n, col=0: pl.BlockSpec((1, tm, n), lambda b, i: (b, i, col))
    full = lambda shape: pl.BlockSpec(shape, lambda b, i: (0,) * len(shape))
    return pl.pallas_call(
        _outproj_kernel,
        grid=(nb, s // tm),
        in_specs=[tok(d), tok(half), tok(half, ob_col),
                  pl.BlockSpec((1, 8, d), _mod_index(nb)),
                  full((1, d)), full((d, d)), full((d, LANES)), full((1, LANES))],
        out_specs=[tok(d), tok(d), tok(LANES)],
        out_shape=[jax.ShapeDtypeStruct((nb, s, d), F32),
                   jax.ShapeDtypeStruct((nb, s, d), BF16),
                   jax.ShapeDtypeStruct((nb, s, LANES), F32)],
        compiler_params=_cp(("parallel", "parallel")),
        name="outproj_route",
    )(x, oa, ob, modl, g, w_out, w_route, b_route)


def _moe_kernel(h_ref, comb_ref, x_ref, mod_ref, w1_ref, w3_ref, w2_ref, o_ref, acc_ref):
    e = pl.program_id(2)

    @pl.when(e == 0)
    def _():
        acc_ref[...] = jnp.zeros_like(acc_ref)

    h = h_ref[0]
    a = _dot(h, w1_ref[0])
    a = a * jax.nn.sigmoid(a) * _dot(h, w3_ref[0])
    y = _dot(a.astype(BF16), w2_ref[0])
    comb = comb_ref[0]
    lane = lax.broadcasted_iota(jnp.int32, comb.shape, 1)
    gate = jnp.sum(jnp.where(lane == e, comb, 0.0), axis=-1, keepdims=True)
    acc_ref[...] += gate * y

    @pl.when(e == pl.num_programs(2) - 1)
    def _():
        o_ref[0] = x_ref[0] + mod_ref[0][5:6] * acc_ref[...]


def moe_dense(h, comb, x, modl, w1, w3, w2):
    nb, s, d = x.shape
    tm = TOKEN_TILE
    n_exp = w1.shape[0]
    tok = lambda n: pl.BlockSpec((1, tm, n), lambda b, i, e: (b, i, 0))
    mod_idx = _mod_index(nb)
    return pl.pallas_call(
        _moe_kernel,
        grid=(nb, s // tm, n_exp),
        in_specs=[tok(d), tok(LANES), tok(d),
                  pl.BlockSpec((1, 8, d), lambda b, i, e: mod_idx(b, i)),
                  pl.BlockSpec((1, d, D_EXPERT), lambda b, i, e: (e, 0, 0)),
                  pl.BlockSpec((1, d, D_EXPERT), lambda b, i, e: (e, 0, 0)),
                  pl.BlockSpec((1, D_EXPERT, d), lambda b, i, e: (e, 0, 0))],
        out_specs=tok(d),
        out_shape=jax.ShapeDtypeStruct((nb, s, d), F32),
        scratch_shapes=[pltpu.VMEM((tm, d), F32)],
        compiler_params=_cp(("parallel", "parallel", "arbitrary")),
        name="moe_dense",
    )(h, comb, x, modl, w1, w3, w2)


def _final_kernel(x_ref, g_ref, o_ref):
    x = x_ref[0]
    ms = jnp.mean(x * x, axis=-1, keepdims=True)
    o_ref[0] = x * lax.rsqrt(ms + RMS_EPS) * g_ref[...]


def final_norm(x, g):
    nb, s, d = x.shape
    tm = TOKEN_TILE
    skip = CTX_LEN // tm
    return pl.pallas_call(
        _final_kernel,
        grid=(nb, (s - CTX_LEN) // tm),
        in_specs=[pl.BlockSpec((1, tm, d), lambda b, i: (b, i + skip, 0)),
                  pl.BlockSpec((1, d), lambda b, i: (0, 0))],
        out_specs=pl.BlockSpec((1, tm, d), lambda b, i: (b, i, 0)),
        out_shape=jax.ShapeDtypeStruct((nb, s - CTX_LEN, d), F32),
        compiler_params=_cp(("parallel", "parallel")),
        name="final_norm",
    )(x, g.reshape(1, d))


def kernel(x, c, ctx, c_ctx, ada_w, ada_b, norm_g, final_g, ev_w_in, ev_w_out, ev_decay_w0, ev_decay_up, ev_iclr_a0, ev_iclr_up, ev_gate_up, ev_k_k, ev_k_a, ev_r_k, ev_lnx_w, ev_lnx_b, ev_conv_w, od_w_in, od_w_out, od_lambda, od_subln, od_rel_bias, moe_wg, moe_bg, moe_we, moe_be, moe_w1, moe_w3, moe_w2):
    nb, seq, d = x.shape
    rows = seq // GRID_W
    depth = ada_w.shape[0]
    assert nb <= 15 and seq % TOKEN_TILE == 0 and ctx.shape[1] == CTX_LEN

    xs = jnp.concatenate([ctx, x], axis=1)
    cond = jnp.zeros((16, d), F32).at[:nb].set(c).at[nb].set(c_ctx)
    mod = ada_all(cond, ada_w, ada_b)
    mod = mod[:, :nb + 1].reshape(depth, nb + 1, 6, d)
    mod = jnp.concatenate([mod, jnp.zeros((depth, nb + 1, 2, d), F32)], axis=2)
    cos_t, sin_t = rope_tables(seq)

    for l in range(depth):
        i = l // 2
        modl = mod[l]
        g1 = norm_g[l, 0].reshape(1, d)
        g2 = norm_g[l, 1].reshape(1, d)
        if l % 2 == 0:
            rkv, lora, gd, conv = inproj_even(xs, modl, g1, ev_w_in[i].astype(BF16))
            xt = rwkv_prep(rkv, lora, ev_decay_w0[i], ev_decay_up[i], ev_iclr_a0[i],
                           ev_iclr_up[i], ev_k_k[i], ev_k_a[i])
            yf, yb = rwkv_scan(xt, rkv)
            o = rwkv_readout(yf, yb, rkv, gd, conv, ev_gate_up[i], ev_r_k[i], ev_lnx_w[i],
                             ev_lnx_b[i], ev_conv_w[i])
            oa, ob, ob_col = o, o, 1
            w_out = ev_w_out[i]
        else:
            dq, dk, dv, nq, nk, nv = inproj_odd(xs, modl, g1, od_w_in[i].astype(BF16), cos_t, sin_t)
            lam_init = 0.8 - 0.6 * math.exp(-0.3 * l)
            oa = diff_attention(dq, dk, dv, od_lambda[i], od_subln[i], lam_init)
            ob = na_attention(nq, nk, nv, na_bias_table(od_rel_bias[i], rows), rows)
            ob_col = 0
            w_out = od_w_out[i]
        w_route = jnp.concatenate(
            [moe_we[l], moe_wg[l], jnp.zeros((d, LANES - N_EXPERTS - N_GROUPS), F32)], axis=1)
        b_route = jnp.concatenate(
            [moe_be[l], moe_bg[l], jnp.zeros((LANES - N_EXPERTS - N_GROUPS,), F32)]).reshape(1, LANES)
        xs, h2, comb = outproj_route(xs, oa, ob, ob_col, modl, g2, w_out.astype(BF16), w_route, b_route)
        xs = moe_dense(h2, comb, xs, modl, moe_w1[l].astype(BF16), moe_w3[l].astype(BF16),
                       moe_w2[l].astype(BF16))
    return final_norm(xs, final_g)
```

```python
import functools
import math

import jax
import jax.numpy as jnp
from jax import lax
from jax.experimental import pallas as pl
from jax.experimental.pallas import tpu as pltpu

F32 = jnp.float32
BF16 = jnp.bfloat16
HI = lax.Precision.HIGHEST

D_MODEL = 1024
DEPTH = 4
GRID_W = 64
CTX_LEN = 256

RWKV_WIDTH = 512
RWKV_HEAD = 64
RWKV_HEADS = 8
DECAY_LORA = 64
ICLR_LORA = 64
GATE_LORA = 128
CONV_WIDTH = 512
LNX_EPS = 64e-5

DIFF_HEADS = 4
DIFF_HEAD = 64
DIFF_VHEAD = 128
DIFF_WIDTH = 512
NA_HEADS = 8
NA_HEAD = 64
NA_WIDTH = 512
NA_MAX_ROWS = 8
NA_COLS = 16
ROPE_BASE = 10000.0
SUBLN_EPS = 1e-5

N_GROUPS = 4
EXPERTS_PER_GROUP = 4
N_EXPERTS = 16
D_EXPERT = 512
RMS_EPS = 1e-6

EVEN_PROJ = 3456
ODD_PROJ = 3072

LANES = 128
TOKEN_TILE = 256
MOE_TILE = 1024
SCAN_TB = 128
SCAN_GROUP = 16
SCAN_VECS = 5
SCAN_PAIRS = RWKV_HEADS // 2
SCAN_ROWS = SCAN_VECS * SCAN_PAIRS * RWKV_HEAD
SCAN_STEP_ROWS = (SCAN_VECS - 1) * SCAN_PAIRS * RWKV_HEAD
SCAN_K = 3 * 2 * SCAN_TB
SCAN_NG = SCAN_TB // SCAN_GROUP
VMEM_LIMIT = 56 * 1024 * 1024
NEG_BIG = -1e30


def _cp(sem, vmem=VMEM_LIMIT):
    return pltpu.CompilerParams(dimension_semantics=sem, vmem_limit_bytes=vmem)


def _dot(a, b):
    return jnp.dot(a, b, preferred_element_type=F32)


def _dot_hi(a, b):
    return jnp.dot(a, b, preferred_element_type=F32, precision=HI)


def _dot_nt(a, b):
    return lax.dot_general(a, b, (((1,), (1,)), ((), ())), preferred_element_type=F32)


def _head_ones(width, head):
    i = lax.broadcasted_iota(jnp.int32, (width, width), 0) // head
    j = lax.broadcasted_iota(jnp.int32, (width, width), 1) // head
    return (i == j).astype(F32)


def _ada_kernel(cond_ref, w_ref, b_ref, o_ref):
    cond = cond_ref[...]
    s = cond * jax.nn.sigmoid(cond)
    o_ref[0] = _dot_hi(s, w_ref[0]) + b_ref[0]


def ada_all(cond, ada_w, ada_b):
    depth, d, n = ada_w.shape
    tn = 1536
    return pl.pallas_call(
        _ada_kernel,
        grid=(depth, n // tn),
        in_specs=[pl.BlockSpec((16, d), lambda l, j: (0, 0)),
                  pl.BlockSpec((1, d, tn), lambda l, j: (l, 0, j)),
                  pl.BlockSpec((1, 1, tn), lambda l, j: (l, 0, j))],
        out_specs=pl.BlockSpec((1, 16, tn), lambda l, j: (l, 0, j)),
        out_shape=jax.ShapeDtypeStruct((depth, 16, n), F32),
        compiler_params=_cp(("parallel", "parallel")),
        name="ada_mod",
    )(cond, ada_w, ada_b.reshape(depth, 1, n))


def _mod_index(n_batch):
    return lambda b, i: (jnp.where(i == 0, n_batch, b), 0, 0)


def _norm_mod(x, g, shift, scale):
    ms = jnp.mean(x * x, axis=-1, keepdims=True)
    return x * lax.rsqrt(ms + RMS_EPS) * g * (1.0 + scale) + shift


def _inproj_even_kernel(x_ref, mod_ref, g_ref, w_ref, rkv_ref, lora_ref, gd_ref, conv_ref):
    m = mod_ref[0]
    h = _norm_mod(x_ref[0], g_ref[...], m[0:1], m[1:2]).astype(BF16)
    p = _dot(h, w_ref[...])
    rkv_ref[0] = p[:, 0:1536]
    lora_ref[0] = p[:, 1536:1792]
    gd_ref[0] = p[:, 1792:1920]
    conv_ref[0] = p[:, 1920:3456]


def inproj_even(x, modl, g, w):
    nb, s, d = x.shape
    tm = TOKEN_TILE
    tok = lambda n: pl.BlockSpec((1, tm, n), lambda b, i: (b, i, 0))
    return pl.pallas_call(
        _inproj_even_kernel,
        grid=(nb, s // tm),
        in_specs=[tok(d),
                  pl.BlockSpec((1, 8, d), _mod_index(nb)),
                  pl.BlockSpec((1, d), lambda b, i: (0, 0)),
                  pl.BlockSpec((d, EVEN_PROJ), lambda b, i: (0, 0))],
        out_specs=[tok(1536), tok(256), tok(128), tok(1536)],
        out_shape=[jax.ShapeDtypeStruct((nb, s, 1536), F32),
                   jax.ShapeDtypeStruct((nb, s, 256), F32),
                   jax.ShapeDtypeStruct((nb, s, 128), F32),
                   jax.ShapeDtypeStruct((nb, s, 1536), F32)],
        compiler_params=_cp(("parallel", "parallel")),
        name="inproj_even",
    )(x, modl, g, w)


def _rope(x, cos, sin_signed):
    lane = lax.broadcasted_iota(jnp.int32, x.shape, 1)
    partner = jnp.where((lane % 32) < 16, pltpu.roll(x, LANES - 16, 1), pltpu.roll(x, 16, 1))
    return x * cos + partner * sin_signed


def _inproj_odd_kernel(x_ref, mod_ref, g_ref, w_ref, cos_ref, sin_ref,
                       dq_ref, dk_ref, dv_ref, nq_ref, nk_ref, nv_ref):
    m = mod_ref[0]
    h = _norm_mod(x_ref[0], g_ref[...], m[0:1], m[1:2]).astype(BF16)
    p = _dot(h, w_ref[...])
    cos = cos_ref[...]
    sin = sin_ref[...]
    for out_ref, base in ((dq_ref, 0), (dk_ref, 512)):
        for c in range(4):
            blk = p[:, base + c * LANES: base + (c + 1) * LANES]
            out_ref[0, :, c * LANES:(c + 1) * LANES] = _rope(blk, cos, sin).astype(BF16)
    dv_ref[0] = p[:, 1024:1536].astype(BF16)
    nq_ref[0] = p[:, 1536:2048].astype(BF16)
    nk_ref[0] = p[:, 2048:2560].astype(BF16)
    nv_ref[0] = p[:, 2560:3072].astype(BF16)


def inproj_odd(x, modl, g, w, cos_t, sin_t):
    nb, s, d = x.shape
    tm = TOKEN_TILE
    tok = lambda n: pl.BlockSpec((1, tm, n), lambda b, i: (b, i, 0))
    tab = pl.BlockSpec((tm, LANES), lambda b, i: (i, 0))
    return pl.pallas_call(
        _inproj_odd_kernel,
        grid=(nb, s // tm),
        in_specs=[tok(d),
                  pl.BlockSpec((1, 8, d), _mod_index(nb)),
                  pl.BlockSpec((1, d), lambda b, i: (0, 0)),
                  pl.BlockSpec((d, ODD_PROJ), lambda b, i: (0, 0)),
                  tab, tab],
        out_specs=[tok(512)] * 6,
        out_shape=[jax.ShapeDtypeStruct((nb, s, 512), BF16)] * 6,
        compiler_params=_cp(("parallel", "parallel")),
        name="inproj_odd",
    )(x, modl, g, w, cos_t, sin_t)


def rope_tables(seq):
    n_freq = DIFF_HEAD // 4
    inv_freq = ROPE_BASE ** (-jnp.arange(n_freq, dtype=F32) / n_freq)
    t = jnp.arange(seq, dtype=jnp.int32)
    pos = jnp.stack([t // GRID_W, t % GRID_W], axis=-1).astype(F32)
    ang = pos[:, :, None] * inv_freq
    cos, sin = jnp.cos(ang), jnp.sin(ang)
    cos64 = jnp.concatenate([cos[:, 0], cos[:, 0], cos[:, 1], cos[:, 1]], axis=-1)
    sin64 = jnp.concatenate([-sin[:, 0], sin[:, 0], -sin[:, 1], sin[:, 1]], axis=-1)
    cos_t = jnp.concatenate([jnp.ones((CTX_LEN, 64), F32), cos64], axis=0)
    sin_t = jnp.concatenate([jnp.zeros((CTX_LEN, 64), F32), sin64], axis=0)
    return jnp.tile(cos_t, (1, 2)), jnp.tile(sin_t, (1, 2))


def _split3(x):
    hi = x.astype(BF16)
    r1 = x - hi.astype(F32)
    mid = r1.astype(BF16)
    lo = (r1 - mid.astype(F32)).astype(BF16)
    return hi, mid, lo


def _prep_kernel(r_ref, k_ref, lora_ref, w0_ref, wup_ref, a0_ref, aup_ref, kk_w_ref, ka_ref, xt_ref):
    k = k_ref[0]
    r = r_ref[0]
    lora = lora_ref[0]
    tb = SCAN_TB
    ones = _head_ones(RWKV_WIDTH, RWKV_HEAD)
    kk = k * kk_w_ref[...]
    ss = _dot_hi(kk * kk, ones)
    kk = kk * lax.rsqrt(jnp.maximum(ss, 1e-24))
    ti = lax.broadcasted_iota(jnp.int32, (tb, tb), 0)
    tj = lax.broadcasted_iota(jnp.int32, (tb, tb), 1)
    same_group = (ti // SCAN_GROUP) == (tj // SCAN_GROUP)
    upto = (jnp.logical_and(same_group, tj <= ti).astype(F32),
            jnp.logical_and(same_group, tj >= ti).astype(F32))

    def put(d, vec, x):
        for term_i, term in enumerate(_split3(x.T)):
            for h in range(RWKV_HEADS):
                row0 = (vec * SCAN_PAIRS + h // 2) * RWKV_HEAD
                lane0 = (term_i * 2 + h % 2) * SCAN_TB
                xt_ref[0, d, 0, row0:row0 + RWKV_HEAD, lane0:lane0 + SCAN_TB] = (
                    term[h * RWKV_HEAD:(h + 1) * RWKV_HEAD, :])

    for d in range(2):
        wd = lora[:, d * 64:(d + 1) * 64]
        ad = lora[:, 128 + d * 64:128 + (d + 1) * 64]
        w = w0_ref[d:d + 1, :] + _dot_hi(jnp.tanh(wd), wup_ref[d])
        log_decay = -math.exp(-0.5) * jax.nn.sigmoid(w)
        a = jax.nn.sigmoid(a0_ref[d:d + 1, :] + _dot_hi(ad, aup_ref[d]))
        log_g = _dot_hi(upto[d], log_decay)
        g = jnp.exp(log_g)
        g_inv = jnp.exp(-log_g)
        put(d, 0, kk * jnp.exp(log_g - log_decay))
        put(d, 1, kk * a * g_inv)
        put(d, 2, k * (1.0 + (a - 1.0) * ka_ref[...]) * g_inv)
        put(d, 3, r * g)
        put(d, 4, g)


def rwkv_prep(rkv, lora, w0, wup, a0, aup, k_k, k_a):
    nb, s, _ = rkv.shape
    tb = SCAN_TB
    c = RWKV_WIDTH
    full = lambda shape: pl.BlockSpec(shape, lambda b, i: (0,) * len(shape))
    return pl.pallas_call(
        _prep_kernel,
        grid=(nb, s // tb),
        in_specs=[pl.BlockSpec((1, tb, c), lambda b, i: (b, i, 0)),
                  pl.BlockSpec((1, tb, c), lambda b, i: (b, i, 1)),
                  pl.BlockSpec((1, tb, 256), lambda b, i: (b, i, 0)),
                  full((2, c)), full((2, DECAY_LORA, c)), full((2, c)), full((2, ICLR_LORA, c)),
                  full((1, c)), full((1, c))],
        out_specs=pl.BlockSpec((1, 2, 1, SCAN_ROWS, SCAN_K), lambda b, i: (b, 0, i, 0, 0)),
        out_shape=jax.ShapeDtypeStruct((nb, 2, s // tb, SCAN_ROWS, SCAN_K), BF16),
        compiler_params=_cp(("parallel", "parallel")),
        name="rwkv_prep",
    )(rkv, rkv, lora, w0, wup, a0, aup, k_k.reshape(1, c), k_a.reshape(1, c))


def scan_selectors():
    tb, grp = SCAN_TB, SCAN_GROUP
    src = jnp.arange(SCAN_K)
    term, head, time = src // (2 * tb), (src // tb) % 2, src % tb
    dst = jnp.arange(SCAN_NG * LANES)
    gather = ((time // grp)[:, None] == (dst // LANES)[None, :]) & (
        (term * 2 * grp + head * grp + time % grp)[:, None] == (dst % LANES)[None, :])
    j = jnp.arange(LANES)
    j_valid, j_head, j_step = j < 3 * 2 * grp, (j // grp) % 2, j % grp
    col = jnp.arange(grp * LANES)
    col_step, col_head = col // LANES, (col % LANES) // RWKV_HEAD
    base = j_valid[:, None] & (j_head[:, None] == col_head[None, :])
    expand = jnp.stack([base & (j_step[:, None] == col_step[None, :]),
                        base & (j_step[:, None] == grp - 1 - col_step[None, :])])
    lane_head = jnp.arange(LANES) // RWKV_HEAD
    base_l = j_valid[:, None] & (j_head[:, None] == lane_head[None, :])
    last = jnp.stack([base_l & (j_step[:, None] == grp - 1), base_l & (j_step[:, None] == 0)])
    return gather.astype(BF16), expand.astype(BF16), last.astype(BF16)


def _scan_kernel(xtf_ref, xtb_ref, vf_ref, vb_ref, gather_ref, expand_ref, last_ref, yf_ref, yb_ref,
                 t_scr, c_scr, big_scr, end_scr):
    tb, grp = SCAN_TB, SCAN_GROUP
    j = pl.program_id(1)

    @pl.when(j == 0)
    def _():
        t_scr[...] = jnp.zeros_like(t_scr)

    xts = (xtf_ref, xtb_ref)
    vs = (vf_ref, vb_ref)
    outs = (yf_ref, yb_ref)
    step_row = lax.broadcasted_iota(jnp.int32, (grp, LANES), 0)

    for d in range(2):
        packed = _dot(xts[d][0], gather_ref[...]).astype(BF16)
        for gi in range(SCAN_NG):
            c_scr[d, gi] = packed[:, gi * LANES:(gi + 1) * LANES]

    def group(g, carry):
        off = g * grp
        for d in range(2):
            lhs = c_scr[d, g if d == 0 else SCAN_NG - 1 - g]
            big_scr[d] = _dot(lhs[0:SCAN_STEP_ROWS], expand_ref[d])
            end_scr[d] = _dot(lhs[SCAN_STEP_ROWS:SCAN_ROWS], last_ref[d])
        for d in range(2):
            base = pl.multiple_of(off if d == 0 else tb - grp - off, grp)
            for p in range(SCAN_PAIRS):
                c = d * SCAN_PAIRS + p
                cols = pl.ds(p * LANES, LANES)
                v_win = vs[d][0, pl.ds(base, grp), cols]
                st = t_scr[c]
                y_win = jnp.zeros((grp, LANES), F32)
                for s in range(grp):
                    i = s if d == 0 else grp - 1 - s
                    col = lambda vec: big_scr[d, (vec * SCAN_PAIRS + p) * RWKV_HEAD:
                                              (vec * SCAN_PAIRS + p + 1) * RWKV_HEAD,
                                              s * LANES:(s + 1) * LANES]
                    s_kk = jnp.sum(st * col(0), axis=0, keepdims=True)
                    st = st - col(1) * s_kk + col(2) * v_win[i:i + 1, :]
                    y = jnp.sum(st * col(3), axis=0, keepdims=True)
                    y_win = jnp.where(step_row == i, y, y_win)
                t_scr[c] = st * end_scr[d, p * RWKV_HEAD:(p + 1) * RWKV_HEAD, :]
                outs[d][0, pl.ds(base, grp), cols] = y_win
        return carry

    lax.fori_loop(0, SCAN_NG, group, 0)


def rwkv_scan(xt, v_src):
    nb, _, nt, _, _ = xt.shape
    s = v_src.shape[1]
    c = RWKV_WIDTH
    tb = SCAN_TB
    nct = CTX_LEN // tb

    def bwd(j):
        return jnp.where(j < nct, nct - 1 - j, nt - 1 - (j - nct))

    gather, expand, last = scan_selectors()
    whole = pl.BlockSpec(memory_space=pltpu.VMEM)
    y_shape = jax.ShapeDtypeStruct((nb, s, c), F32)
    return pl.pallas_call(
        _scan_kernel,
        grid=(nb, nt),
        in_specs=[pl.BlockSpec((1, None, None, SCAN_ROWS, SCAN_K), lambda b, j: (b, 0, j, 0, 0)),
                  pl.BlockSpec((1, None, None, SCAN_ROWS, SCAN_K), lambda b, j: (b, 1, bwd(j), 0, 0)),
                  pl.BlockSpec((1, tb, c), lambda b, j: (b, j, 2)),
                  pl.BlockSpec((1, tb, c), lambda b, j: (b, bwd(j), 2)),
                  whole, whole, whole],
        out_specs=[pl.BlockSpec((1, tb, c), lambda b, j: (b, j, 0)),
                   pl.BlockSpec((1, tb, c), lambda b, j: (b, bwd(j), 0))],
        out_shape=[y_shape, y_shape],
        scratch_shapes=[pltpu.VMEM((2 * SCAN_PAIRS, RWKV_HEAD, LANES), F32),
                        pltpu.VMEM((2, SCAN_NG, SCAN_ROWS, LANES), BF16),
                        pltpu.VMEM((2, SCAN_STEP_ROWS, SCAN_GROUP * LANES), F32),
                        pltpu.VMEM((2, SCAN_PAIRS * RWKV_HEAD, LANES), F32)],
        compiler_params=_cp(("parallel", "arbitrary")),
        name="rwkv_scan",
    )(xt, xt, v_src, v_src, gather, expand, last)


def _readout_kernel(yf_ref, yb_ref, r_ref, k_ref, v_ref, gd_ref,
                    cb_ref, cc_ref, cx_ref, pc_ref, px_ref, nc_ref, nx_ref,
                    gup_ref, rk_ref, lw_ref, lb_ref, cw_ref, o_ref):
    i = pl.program_id(1)
    n_tiles = pl.num_programs(1)
    ones = _head_ones(RWKV_WIDTH, RWKV_HEAD)
    y = yf_ref[0] + yb_ref[0]
    mean = _dot_hi(y, ones) * (1.0 / RWKV_HEAD)
    yc = y - mean
    var = _dot_hi(yc * yc, ones) * (1.0 / RWKV_HEAD)
    out = yc * lax.rsqrt(var + LNX_EPS) * lw_ref[...] + lb_ref[...]
    bonus = _dot_hi(r_ref[0] * k_ref[0] * rk_ref[...], ones) * v_ref[0]
    gate = _dot_hi(jax.nn.sigmoid(gd_ref[0]), gup_ref[...])
    o_ref[0, :, 0:RWKV_WIDTH] = ((out + bonus) * gate).astype(BF16)

    u = cc_ref[0] * cx_ref[0]
    tm = u.shape[0]
    row = lax.broadcasted_iota(jnp.int32, u.shape, 0)
    has_prev = jnp.logical_and(i != 0, i != CTX_LEN // tm)
    has_next = jnp.logical_and(i != CTX_LEN // tm - 1, i != n_tiles - 1)
    u_prev_edge = jnp.where(has_prev, pc_ref[0, 7:8, :] * px_ref[0, 7:8, :], 0.0)
    u_next_edge = jnp.where(has_next, nc_ref[0, 0:1, :] * nx_ref[0, 0:1, :], 0.0)
    u_prev = jnp.where(row == 0, u_prev_edge, pltpu.roll(u, 1, 0))
    u_next = jnp.where(row == tm - 1, u_next_edge, pltpu.roll(u, tm - 1, 0))
    conv = u_prev * cw_ref[0:1, :] + u * cw_ref[1:2, :] + u_next * cw_ref[2:3, :]
    o_ref[0, :, RWKV_WIDTH:] = (cb_ref[0] * conv).astype(BF16)


def rwkv_readout(yf, yb, rkv, gd, conv, g_up, r_k, lnx_w, lnx_b, conv_w):
    nb, s, c = yf.shape
    tm = TOKEN_TILE
    n_tiles = s // tm
    per8 = tm // 8
    last8 = s // 8 - 1
    tok = lambda col, n=c: pl.BlockSpec((1, tm, n), lambda b, i: (b, i, col))
    prev = lambda col: pl.BlockSpec((1, 8, c), lambda b, i: (b, jnp.maximum(i * per8 - 1, 0), col))
    nxt = lambda col: pl.BlockSpec((1, 8, c), lambda b, i: (b, jnp.minimum((i + 1) * per8, last8), col))
    full = lambda shape: pl.BlockSpec(shape, lambda b, i: (0,) * len(shape))
    return pl.pallas_call(
        _readout_kernel,
        grid=(nb, n_tiles),
        in_specs=[tok(0), tok(0), tok(0), tok(1), tok(2), tok(0, 128),
                  tok(0), tok(1), tok(2), prev(1), prev(2), nxt(1), nxt(2),
                  full((GATE_LORA, c)), full((1, c)), full((1, c)), full((1, c)), full((3, c))],
        out_specs=pl.BlockSpec((1, tm, 2 * c), lambda b, i: (b, i, 0)),
        out_shape=jax.ShapeDtypeStruct((nb, s, 2 * c), BF16),
        compiler_params=_cp(("parallel", "parallel")),
        name="rwkv_readout",
    )(yf, yb, rkv, rkv, rkv, gd, conv, conv, conv, conv, conv, conv, conv,
      g_up, r_k.reshape(1, c), lnx_w.reshape(1, c), lnx_b.reshape(1, c), conv_w)


def _softmax_rows(s):
    m = jnp.max(s, axis=-1, keepdims=True)
    e = jnp.exp(s - m)
    return e / jnp.sum(e, axis=-1, keepdims=True)


def _diff_kernel(q_ref, k_ref, v_ref, lam_ref, sub_ref, o_ref, *, lam_init):
    i = pl.program_id(2)
    lam = lam_ref[...]
    lam_full = (jnp.exp(jnp.sum(lam[0:1] * lam[1:2], axis=1, keepdims=True))
                - jnp.exp(jnp.sum(lam[2:3] * lam[3:4], axis=1, keepdims=True)) + lam_init)
    q = q_ref[0] * (DIFF_HEAD ** -0.5)
    lane = lax.broadcasted_iota(jnp.int32, q.shape, 1)
    q1 = jnp.where(lane < DIFF_HEAD, q, jnp.zeros_like(q))
    q2 = jnp.where(lane < DIFF_HEAD, jnp.zeros_like(q), q)

    def attend(n_keys):
        k = k_ref[0, 0:n_keys, :]
        v = v_ref[0, 0:n_keys, :]

        def softmax_times_v(qm):
            s = _dot_nt(qm, k)
            e = jnp.exp(s - jnp.max(s, axis=-1, keepdims=True))
            return _dot(e.astype(BF16), v) / jnp.sum(e, axis=-1, keepdims=True)

        o = softmax_times_v(q1) - lam_full * softmax_times_v(q2)
        ms = jnp.mean(o * o, axis=-1, keepdims=True)
        o = o * lax.rsqrt(ms + SUBLN_EPS) * sub_ref[...] * (1.0 - lam_init)
        o_ref[0] = o.astype(BF16)

    @pl.when(i == 0)
    def _():
        attend(CTX_LEN)

    @pl.when(i != 0)
    def _():
        attend(k_ref.shape[1])


def diff_attention(dq, dk, dv, lam, subln, lam_init):
    nb, s, _ = dq.shape
    tm = TOKEN_TILE
    return pl.pallas_call(
        functools.partial(_diff_kernel, lam_init=lam_init),
        grid=(nb, DIFF_HEADS, s // tm),
        in_specs=[pl.BlockSpec((1, tm, LANES), lambda b, h, i: (b, i, h)),
                  pl.BlockSpec((1, s, LANES), lambda b, h, i: (b, 0, h)),
                  pl.BlockSpec((1, s, LANES), lambda b, h, i: (b, 0, h)),
                  pl.BlockSpec((4, DIFF_HEAD), lambda b, h, i: (0, 0)),
                  pl.BlockSpec((1, DIFF_VHEAD), lambda b, h, i: (0, 0))],
        out_specs=pl.BlockSpec((1, tm, LANES), lambda b, h, i: (b, i, h)),
        out_shape=jax.ShapeDtypeStruct((nb, s, DIFF_WIDTH), BF16),
        compiler_params=_cp(("parallel", "parallel", "parallel")),
        name="diff_attention",
    )(dq, dk, dv, lam, subln.reshape(1, DIFF_VHEAD))


def na_bias_table(rel_bias, rows):
    kr = min(NA_MAX_ROWS, rows)
    cols = jnp.arange(GRID_W)
    col_start = jnp.clip(cols - NA_COLS // 2, 0, GRID_W - NA_COLS)
    kc = jnp.arange(GRID_W)
    inside = (kc[None, :] >= col_start[:, None]) & (kc[None, :] < col_start[:, None] + NA_COLS)
    col_off = kc[None, :] - cols[:, None] + (NA_COLS - 1)
    row_off = jnp.arange(kr)[None, :] - jnp.arange(NA_MAX_ROWS)[:, None] + (NA_MAX_ROWS - 1)
    pick_row = (row_off[:, :, None] == jnp.arange(2 * NA_MAX_ROWS - 1)).astype(F32)
    pick_col = ((col_off[:, :, None] == jnp.arange(2 * NA_COLS - 1)) & inside[:, :, None]).astype(F32)
    b = jnp.einsum('hro,pjr,cko->hpcjk', rel_bias.astype(F32), pick_row, pick_col, precision=HI)
    b = jnp.where(inside[None, None, :, None, :], b, NEG_BIG)
    return b.reshape(rel_bias.shape[0], NA_MAX_ROWS, GRID_W, kr * GRID_W)


def _na_kernel(q_ref, k_ref, v_ref, bias_ref, o_ref, *, rows):
    kr = min(NA_MAX_ROWS, rows)
    scale = NA_HEAD ** -0.5
    w = GRID_W
    lane_q = lax.broadcasted_iota(jnp.int32, (2 * w, LANES), 1)
    row_q = lax.broadcasted_iota(jnp.int32, (2 * w, LANES), 0)
    own = (lane_q < NA_HEAD) == (row_q < w)
    first_head = lax.broadcasted_iota(jnp.int32, (w, LANES), 1) < NA_HEAD
    k_ctx = k_ref[0, 0:CTX_LEN, :]
    v_ctx = v_ref[0, 0:CTX_LEN, :]

    def two_heads(q):
        q2 = jnp.concatenate([q, q], axis=0)
        return jnp.where(own, q2, jnp.zeros_like(q2))

    def merge(o):
        return jnp.where(first_head, o[0:w], o[w:2 * w])

    def body(r, carry):
        r_start = jnp.clip(r - kr // 2, 0, rows - kr)
        q = two_heads(q_ref[0, pl.ds(pl.multiple_of(CTX_LEN + r * w, w), w), :])
        base = pl.multiple_of(CTX_LEN + r_start * w, w)
        k_win = k_ref[0, pl.ds(base, kr * w), :]
        v_win = v_ref[0, pl.ds(base, kr * w), :]
        place = r - r_start
        bias = jnp.concatenate([bias_ref[0, place], bias_ref[1, place]], axis=0)
        s_loc = _dot_nt(q, k_win) * scale + bias
        s_ctx = _dot_nt(q, k_ctx) * scale
        m = jnp.maximum(jnp.max(s_loc, axis=-1, keepdims=True), jnp.max(s_ctx, axis=-1, keepdims=True))
        e_loc = jnp.exp(s_loc - m)
        e_ctx = jnp.exp(s_ctx - m)
        z = jnp.sum(e_loc, axis=-1, keepdims=True) + jnp.sum(e_ctx, axis=-1, keepdims=True)
        o = (_dot((e_ctx / z).astype(BF16), v_ctx) + _dot((e_loc / z).astype(BF16), v_win))
        o_ref[0, pl.ds(pl.multiple_of(CTX_LEN + r * w, w), w), :] = merge(o).astype(BF16)
        return carry

    lax.fori_loop(0, rows, body, 0, unroll=2)

    for blk in range(CTX_LEN // w):
        q = two_heads(q_ref[0, blk * w:(blk + 1) * w, :])
        p = _softmax_rows(_dot_nt(q, k_ctx) * scale)
        o_ref[0, blk * w:(blk + 1) * w, :] = merge(_dot(p.astype(BF16), v_ctx)).astype(BF16)


def na_attention(nq, nk, nv, bias_tab, rows):
    nb, s, _ = nq.shape
    kr = min(NA_MAX_ROWS, rows)
    seq = pl.BlockSpec((1, s, LANES), lambda b, p: (b, 0, p))
    return pl.pallas_call(
        functools.partial(_na_kernel, rows=rows),
        grid=(nb, NA_HEADS // 2),
        in_specs=[seq, seq, seq,
                  pl.BlockSpec((2, NA_MAX_ROWS, GRID_W, kr * GRID_W), lambda b, p: (p, 0, 0, 0))],
        out_specs=seq,
        out_shape=jax.ShapeDtypeStruct((nb, s, NA_WIDTH), BF16),
        compiler_params=_cp(("parallel", "parallel")),
        name="na_attention",
    )(nq, nk, nv, bias_tab)


def _route(logits):
    lane = lax.broadcasted_iota(jnp.int32, logits.shape, 1)
    big = jnp.int32(LANES)
    is_g = jnp.logical_and(lane >= N_EXPERTS, lane < N_EXPERTS + N_GROUPS)
    lg = jnp.where(is_g, logits, NEG_BIG)
    mg = jnp.max(lg, axis=-1, keepdims=True)
    zg = jnp.sum(jnp.where(is_g, jnp.exp(lg - mg), 0.0), axis=-1, keepdims=True)
    pg_top = 1.0 / zg
    g_sel = jnp.min(jnp.where(jnp.logical_and(is_g, lg == mg), lane, big), axis=-1, keepdims=True) - N_EXPERTS
    in_g = jnp.logical_and(lane < N_EXPERTS, (lane // EXPERTS_PER_GROUP) == g_sel)
    le = jnp.where(in_g, logits, NEG_BIG)
    m1 = jnp.max(le, axis=-1, keepdims=True)
    i1 = jnp.min(jnp.where(jnp.logical_and(in_g, le == m1), lane, big), axis=-1, keepdims=True)
    rest = jnp.logical_and(in_g, lane != i1)
    le2 = jnp.where(rest, logits, NEG_BIG)
    m2 = jnp.max(le2, axis=-1, keepdims=True)
    i2 = jnp.min(jnp.where(jnp.logical_and(rest, le2 == m2), lane, big), axis=-1, keepdims=True)
    e2 = jnp.exp(m2 - m1)
    w1 = pg_top / (1.0 + e2)
    w2 = pg_top * e2 / (1.0 + e2)
    return jnp.where(lane == i1, w1, 0.0) + jnp.where(lane == i2, w2, 0.0)


def _outproj_kernel(x_ref, oa_ref, ob_ref, mod_ref, g_ref, w_ref, wr_ref, br_ref,
                    xo_ref, h_ref, comb_ref):
    m = mod_ref[0]
    half = oa_ref.shape[2]
    y = _dot(oa_ref[0], w_ref[0:half, :]) + _dot(ob_ref[0], w_ref[half:, :])
    x = x_ref[0] + m[2:3] * y
    xo_ref[0] = x
    h = _norm_mod(x, g_ref[...], m[3:4], m[4:5])
    h_ref[0] = h.astype(BF16)
    comb_ref[0] = _route(_dot_hi(h, wr_ref[...]) + br_ref[...])


def outproj_route(x, oa, ob, ob_col, modl, g, w_out, w_route, b_route):
    nb, s, d = x.shape
    tm = TOKEN_TILE
    half = d // 2
    tok = lambda n, col=0: pl.BlockSpec((1, tm, n), lambda b, i: (b, i, col))
    full = lambda shape: pl.BlockSpec(shape, lambda b, i: (0,) * len(shape))
    return pl.pallas_call(
        _outproj_kernel,
        grid=(nb, s // tm),
        in_specs=[tok(d), tok(half), tok(half, ob_col),
                  pl.BlockSpec((1, 8, d), _mod_index(nb)),
                  full((1, d)), full((d, d)), full((d, LANES)), full((1, LANES))],
        out_specs=[tok(d), tok(d), tok(LANES)],
        out_shape=[jax.ShapeDtypeStruct((nb, s, d), F32),
                   jax.ShapeDtypeStruct((nb, s, d), BF16),
                   jax.ShapeDtypeStruct((nb, s, LANES), F32)],
        compiler_params=_cp(("parallel", "parallel")),
        name="outproj_route",
    )(x, oa, ob, modl, g, w_out, w_route, b_route)


def _moe_kernel(h_ref, comb_ref, x_ref, mod_ref, w1_ref, w3_ref, w2_ref, o_ref, acc_ref, *,
                n_batch, tiles_per_row):
    t = pl.program_id(0)
    e = pl.program_id(1)

    @pl.when(e == 0)
    def _():
        acc_ref[...] = jnp.zeros_like(acc_ref)

    h = h_ref[...]
    a = _dot(h, w1_ref[0])
    a = a * jax.nn.sigmoid(a) * _dot(h, w3_ref[0])
    y = _dot(a.astype(BF16), w2_ref[0])
    comb = comb_ref[...]
    lane = lax.broadcasted_iota(jnp.int32, comb.shape, 1)
    gate = jnp.sum(jnp.where(lane == e, comb, 0.0), axis=-1, keepdims=True)
    acc_ref[...] += gate * y

    @pl.when(e == pl.num_programs(1) - 1)
    def _():
        per_tile = MOE_TILE // TOKEN_TILE
        for q in range(per_tile):
            blk = t * per_tile + q
            b = lax.div(blk, tiles_per_row)
            first = lax.rem(blk, tiles_per_row) == 0
            gate2 = mod_ref[jnp.where(first, n_batch, b)][5:6]
            rows = slice(q * TOKEN_TILE, (q + 1) * TOKEN_TILE)
            o_ref[rows, :] = x_ref[rows, :] + gate2 * acc_ref[rows, :]


def moe_dense(h, comb, x, modl, w1, w3, w2):
    nb, s, d = x.shape
    tm = MOE_TILE
    n_exp = w1.shape[0]
    n_tok = nb * s
    assert n_tok % tm == 0 and tm % TOKEN_TILE == 0
    tok = lambda n: pl.BlockSpec((tm, n), lambda t, e: (t, 0))
    out = pl.pallas_call(
        functools.partial(_moe_kernel, n_batch=nb, tiles_per_row=s // TOKEN_TILE),
        grid=(n_tok // tm, n_exp),
        in_specs=[tok(d), tok(LANES), tok(d),
                  pl.BlockSpec(memory_space=pltpu.VMEM),
                  pl.BlockSpec((1, d, D_EXPERT), lambda t, e: (e, 0, 0)),
                  pl.BlockSpec((1, d, D_EXPERT), lambda t, e: (e, 0, 0)),
                  pl.BlockSpec((1, D_EXPERT, d), lambda t, e: (e, 0, 0))],
        out_specs=tok(d),
        out_shape=jax.ShapeDtypeStruct((n_tok, d), F32),
        scratch_shapes=[pltpu.VMEM((tm, d), F32)],
        compiler_params=_cp(("parallel", "arbitrary")),
        name="moe_dense",
    )(h.reshape(n_tok, d), comb.reshape(n_tok, LANES), x.reshape(n_tok, d), modl, w1, w3, w2)
    return out.reshape(nb, s, d)


def _final_kernel(x_ref, g_ref, o_ref):
    x = x_ref[0]
    ms = jnp.mean(x * x, axis=-1, keepdims=True)
    o_ref[0] = x * lax.rsqrt(ms + RMS_EPS) * g_ref[...]


def final_norm(x, g):
    nb, s, d = x.shape
    tm = TOKEN_TILE
    skip = CTX_LEN // tm
    return pl.pallas_call(
        _final_kernel,
        grid=(nb, (s - CTX_LEN) // tm),
        in_specs=[pl.BlockSpec((1, tm, d), lambda b, i: (b, i + skip, 0)),
                  pl.BlockSpec((1, d), lambda b, i: (0, 0))],
        out_specs=pl.BlockSpec((1, tm, d), lambda b, i: (b, i, 0)),
        out_shape=jax.ShapeDtypeStruct((nb, s - CTX_LEN, d), F32),
        compiler_params=_cp(("parallel", "parallel")),
        name="final_norm",
    )(x, g.reshape(1, d))


def kernel(x, c, ctx, c_ctx, ada_w, ada_b, norm_g, final_g, ev_w_in, ev_w_out, ev_decay_w0, ev_decay_up, ev_iclr_a0, ev_iclr_up, ev_gate_up, ev_k_k, ev_k_a, ev_r_k, ev_lnx_w, ev_lnx_b, ev_conv_w, od_w_in, od_w_out, od_lambda, od_subln, od_rel_bias, moe_wg, moe_bg, moe_we, moe_be, moe_w1, moe_w3, moe_w2):
    nb, seq, d = x.shape
    rows = seq // GRID_W
    depth = ada_w.shape[0]
    assert nb <= 15 and seq % TOKEN_TILE == 0 and ctx.shape[1] == CTX_LEN

    xs = jnp.concatenate([ctx, x], axis=1)
    cond = jnp.zeros((16, d), F32).at[:nb].set(c).at[nb].set(c_ctx)
    mod = ada_all(cond, ada_w, ada_b)
    mod = mod[:, :nb + 1].reshape(depth, nb + 1, 6, d)
    mod = jnp.concatenate([mod, jnp.zeros((depth, nb + 1, 2, d), F32)], axis=2)
    cos_t, sin_t = rope_tables(seq)

    for l in range(depth):
        i = l // 2
        modl = mod[l]
        g1 = norm_g[l, 0].reshape(1, d)
        g2 = norm_g[l, 1].reshape(1, d)
        if l % 2 == 0:
            rkv, lora, gd, conv = inproj_even(xs, modl, g1, ev_w_in[i].astype(BF16))
            xt = rwkv_prep(rkv, lora, ev_decay_w0[i], ev_decay_up[i], ev_iclr_a0[i],
                           ev_iclr_up[i], ev_k_k[i], ev_k_a[i])
            yf, yb = rwkv_scan(xt, rkv)
            o = rwkv_readout(yf, yb, rkv, gd, conv, ev_gate_up[i], ev_r_k[i], ev_lnx_w[i],
                             ev_lnx_b[i], ev_conv_w[i])
            oa, ob, ob_col = o, o, 1
            w_out = ev_w_out[i]
        else:
            dq, dk, dv, nq, nk, nv = inproj_odd(xs, modl, g1, od_w_in[i].astype(BF16), cos_t, sin_t)
            lam_init = 0.8 - 0.6 * math.exp(-0.3 * l)
            oa = diff_attention(dq, dk, dv, od_lambda[i], od_subln[i], lam_init)
            ob = na_attention(nq, nk, nv, na_bias_table(od_rel_bias[i], rows), rows)
            ob_col = 0
            w_out = od_w_out[i]
        w_route = jnp.concatenate(
            [moe_we[l], moe_wg[l], jnp.zeros((d, LANES - N_EXPERTS - N_GROUPS), F32)], axis=1)
        b_route = jnp.concatenate(
            [moe_be[l], moe_bg[l], jnp.zeros((LANES - N_EXPERTS - N_GROUPS,), F32)]).reshape(1, LANES)
        xs, h2, comb = outproj_route(xs, oa, ob, ob_col, modl, g2, w_out.astype(BF16), w_route, b_route)
        xs = moe_dense(h2, comb, xs, modl, moe_w1[l].astype(BF16), moe_w3[l].astype(BF16),
                       moe_w2[l].astype(BF16))
    return final_norm(xs, final_g)
```

```python
import functools
import math

import jax
import jax.numpy as jnp
from jax import lax
from jax.experimental import pallas as pl
from jax.experimental.pallas import tpu as pltpu

F32 = jnp.float32
BF16 = jnp.bfloat16
HI = lax.Precision.HIGHEST

D_MODEL = 1024
DEPTH = 4
GRID_W = 64
CTX_LEN = 256

RWKV_WIDTH = 512
RWKV_HEAD = 64
RWKV_HEADS = 8
DECAY_LORA = 64
ICLR_LORA = 64
GATE_LORA = 128
CONV_WIDTH = 512
LNX_EPS = 64e-5

DIFF_HEADS = 4
DIFF_HEAD = 64
DIFF_VHEAD = 128
DIFF_WIDTH = 512
NA_HEADS = 8
NA_HEAD = 64
NA_WIDTH = 512
NA_MAX_ROWS = 8
NA_COLS = 16
ROPE_BASE = 10000.0
SUBLN_EPS = 1e-5

N_GROUPS = 4
EXPERTS_PER_GROUP = 4
N_EXPERTS = 16
D_EXPERT = 512
RMS_EPS = 1e-6

EVEN_PROJ = 3456
ODD_PROJ = 3072

LANES = 128
TOKEN_TILE = 256
MOE_ROWS = 256
SCAN_TB = 128
SCAN_GROUP = 16
SCAN_VECS = 5
SCAN_PAIRS = RWKV_HEADS // 2
SCAN_ROWS = SCAN_VECS * SCAN_PAIRS * RWKV_HEAD
SCAN_STEP_ROWS = (SCAN_VECS - 1) * SCAN_PAIRS * RWKV_HEAD
SCAN_K = 3 * 2 * SCAN_TB
SCAN_NG = SCAN_TB // SCAN_GROUP
VMEM_LIMIT = 56 * 1024 * 1024
NEG_BIG = -1e30


def _cp(sem, vmem=VMEM_LIMIT):
    return pltpu.CompilerParams(dimension_semantics=sem, vmem_limit_bytes=vmem)


def _dot(a, b):
    return jnp.dot(a, b, preferred_element_type=F32)


def _dot_hi(a, b):
    return jnp.dot(a, b, preferred_element_type=F32, precision=HI)


def _dot_nt(a, b):
    return lax.dot_general(a, b, (((1,), (1,)), ((), ())), preferred_element_type=F32)


def _head_ones(width, head):
    i = lax.broadcasted_iota(jnp.int32, (width, width), 0) // head
    j = lax.broadcasted_iota(jnp.int32, (width, width), 1) // head
    return (i == j).astype(F32)


def _ada_kernel(cond_ref, w_ref, b_ref, o_ref):
    cond = cond_ref[...]
    s = cond * jax.nn.sigmoid(cond)
    o_ref[0] = _dot_hi(s, w_ref[0]) + b_ref[0]


def ada_all(cond, ada_w, ada_b):
    depth, d, n = ada_w.shape
    tn = 1536
    return pl.pallas_call(
        _ada_kernel,
        grid=(depth, n // tn),
        in_specs=[pl.BlockSpec((16, d), lambda l, j: (0, 0)),
                  pl.BlockSpec((1, d, tn), lambda l, j: (l, 0, j)),
                  pl.BlockSpec((1, 1, tn), lambda l, j: (l, 0, j))],
        out_specs=pl.BlockSpec((1, 16, tn), lambda l, j: (l, 0, j)),
        out_shape=jax.ShapeDtypeStruct((depth, 16, n), F32),
        compiler_params=_cp(("parallel", "parallel")),
        name="ada_mod",
    )(cond, ada_w, ada_b.reshape(depth, 1, n))


def _mod_index(n_batch):
    return lambda b, i: (jnp.where(i == 0, n_batch, b), 0, 0)


def _norm_mod(x, g, shift, scale):
    ms = jnp.mean(x * x, axis=-1, keepdims=True)
    return x * lax.rsqrt(ms + RMS_EPS) * g * (1.0 + scale) + shift


def _inproj_even_kernel(x_ref, mod_ref, g_ref, w_ref, rkv_ref, lora_ref, gd_ref, conv_ref):
    m = mod_ref[0]
    h = _norm_mod(x_ref[0], g_ref[...], m[0:1], m[1:2]).astype(BF16)
    p = _dot(h, w_ref[...])
    rkv_ref[0] = p[:, 0:1536]
    lora_ref[0] = p[:, 1536:1792]
    gd_ref[0] = p[:, 1792:1920]
    conv_ref[0] = p[:, 1920:3456]


def inproj_even(x, modl, g, w):
    nb, s, d = x.shape
    tm = TOKEN_TILE
    tok = lambda n: pl.BlockSpec((1, tm, n), lambda b, i: (b, i, 0))
    return pl.pallas_call(
        _inproj_even_kernel,
        grid=(nb, s // tm),
        in_specs=[tok(d),
                  pl.BlockSpec((1, 8, d), _mod_index(nb)),
                  pl.BlockSpec((1, d), lambda b, i: (0, 0)),
                  pl.BlockSpec((d, EVEN_PROJ), lambda b, i: (0, 0))],
        out_specs=[tok(1536), tok(256), tok(128), tok(1536)],
        out_shape=[jax.ShapeDtypeStruct((nb, s, 1536), F32),
                   jax.ShapeDtypeStruct((nb, s, 256), F32),
                   jax.ShapeDtypeStruct((nb, s, 128), F32),
                   jax.ShapeDtypeStruct((nb, s, 1536), F32)],
        compiler_params=_cp(("parallel", "parallel")),
        name="inproj_even",
    )(x, modl, g, w)


def _rope(x, cos, sin_signed):
    lane = lax.broadcasted_iota(jnp.int32, x.shape, 1)
    partner = jnp.where((lane % 32) < 16, pltpu.roll(x, LANES - 16, 1), pltpu.roll(x, 16, 1))
    return x * cos + partner * sin_signed


def _inproj_odd_kernel(x_ref, mod_ref, g_ref, w_ref, cos_ref, sin_ref,
                       dq_ref, dk_ref, dv_ref, nq_ref, nk_ref, nv_ref):
    m = mod_ref[0]
    h = _norm_mod(x_ref[0], g_ref[...], m[0:1], m[1:2]).astype(BF16)
    p = _dot(h, w_ref[...])
    cos = cos_ref[...]
    sin = sin_ref[...]
    for out_ref, base in ((dq_ref, 0), (dk_ref, 512)):
        for c in range(4):
            blk = p[:, base + c * LANES: base + (c + 1) * LANES]
            out_ref[0, :, c * LANES:(c + 1) * LANES] = _rope(blk, cos, sin).astype(BF16)
    dv_ref[0] = p[:, 1024:1536].astype(BF16)
    nq_ref[0] = p[:, 1536:2048].astype(BF16)
    nk_ref[0] = p[:, 2048:2560].astype(BF16)
    nv_ref[0] = p[:, 2560:3072].astype(BF16)


def inproj_odd(x, modl, g, w, cos_t, sin_t):
    nb, s, d = x.shape
    tm = TOKEN_TILE
    tok = lambda n: pl.BlockSpec((1, tm, n), lambda b, i: (b, i, 0))
    tab = pl.BlockSpec((tm, LANES), lambda b, i: (i, 0))
    return pl.pallas_call(
        _inproj_odd_kernel,
        grid=(nb, s // tm),
        in_specs=[tok(d),
                  pl.BlockSpec((1, 8, d), _mod_index(nb)),
                  pl.BlockSpec((1, d), lambda b, i: (0, 0)),
                  pl.BlockSpec((d, ODD_PROJ), lambda b, i: (0, 0)),
                  tab, tab],
        out_specs=[tok(512)] * 6,
        out_shape=[jax.ShapeDtypeStruct((nb, s, 512), BF16)] * 6,
        compiler_params=_cp(("parallel", "parallel")),
        name="inproj_odd",
    )(x, modl, g, w, cos_t, sin_t)


def rope_tables(seq):
    n_freq = DIFF_HEAD // 4
    inv_freq = ROPE_BASE ** (-jnp.arange(n_freq, dtype=F32) / n_freq)
    t = jnp.arange(seq, dtype=jnp.int32)
    pos = jnp.stack([t // GRID_W, t % GRID_W], axis=-1).astype(F32)
    ang = pos[:, :, None] * inv_freq
    cos, sin = jnp.cos(ang), jnp.sin(ang)
    cos64 = jnp.concatenate([cos[:, 0], cos[:, 0], cos[:, 1], cos[:, 1]], axis=-1)
    sin64 = jnp.concatenate([-sin[:, 0], sin[:, 0], -sin[:, 1], sin[:, 1]], axis=-1)
    cos_t = jnp.concatenate([jnp.ones((CTX_LEN, 64), F32), cos64], axis=0)
    sin_t = jnp.concatenate([jnp.zeros((CTX_LEN, 64), F32), sin64], axis=0)
    return jnp.tile(cos_t, (1, 2)), jnp.tile(sin_t, (1, 2))


def _split3(x):
    hi = x.astype(BF16)
    r1 = x - hi.astype(F32)
    mid = r1.astype(BF16)
    lo = (r1 - mid.astype(F32)).astype(BF16)
    return hi, mid, lo


def _prep_kernel(r_ref, k_ref, lora_ref, w0_ref, wup_ref, a0_ref, aup_ref, kk_w_ref, ka_ref, xt_ref):
    k = k_ref[0]
    r = r_ref[0]
    lora = lora_ref[0]
    tb = SCAN_TB
    ones = _head_ones(RWKV_WIDTH, RWKV_HEAD)
    kk = k * kk_w_ref[...]
    ss = _dot_hi(kk * kk, ones)
    kk = kk * lax.rsqrt(jnp.maximum(ss, 1e-24))
    ti = lax.broadcasted_iota(jnp.int32, (tb, tb), 0)
    tj = lax.broadcasted_iota(jnp.int32, (tb, tb), 1)
    same_group = (ti // SCAN_GROUP) == (tj // SCAN_GROUP)
    upto = (jnp.logical_and(same_group, tj <= ti).astype(F32),
            jnp.logical_and(same_group, tj >= ti).astype(F32))

    def put(d, vec, x):
        for term_i, term in enumerate(_split3(x.T)):
            for h in range(RWKV_HEADS):
                row0 = (vec * SCAN_PAIRS + h // 2) * RWKV_HEAD
                lane0 = (term_i * 2 + h % 2) * SCAN_TB
                xt_ref[0, d, 0, row0:row0 + RWKV_HEAD, lane0:lane0 + SCAN_TB] = (
                    term[h * RWKV_HEAD:(h + 1) * RWKV_HEAD, :])

    for d in range(2):
        wd = lora[:, d * 64:(d + 1) * 64]
        ad = lora[:, 128 + d * 64:128 + (d + 1) * 64]
        w = w0_ref[d:d + 1, :] + _dot_hi(jnp.tanh(wd), wup_ref[d])
        log_decay = -math.exp(-0.5) * jax.nn.sigmoid(w)
        a = jax.nn.sigmoid(a0_ref[d:d + 1, :] + _dot_hi(ad, aup_ref[d]))
        log_g = _dot_hi(upto[d], log_decay)
        g = jnp.exp(log_g)
        g_inv = jnp.exp(-log_g)
        put(d, 0, kk * jnp.exp(log_g - log_decay))
        put(d, 1, kk * a * g_inv)
        put(d, 2, k * (1.0 + (a - 1.0) * ka_ref[...]) * g_inv)
        put(d, 3, r * g)
        put(d, 4, g)


def rwkv_prep(rkv, lora, w0, wup, a0, aup, k_k, k_a):
    nb, s, _ = rkv.shape
    tb = SCAN_TB
    c = RWKV_WIDTH
    full = lambda shape: pl.BlockSpec(shape, lambda b, i: (0,) * len(shape))
    return pl.pallas_call(
        _prep_kernel,
        grid=(nb, s // tb),
        in_specs=[pl.BlockSpec((1, tb, c), lambda b, i: (b, i, 0)),
                  pl.BlockSpec((1, tb, c), lambda b, i: (b, i, 1)),
                  pl.BlockSpec((1, tb, 256), lambda b, i: (b, i, 0)),
                  full((2, c)), full((2, DECAY_LORA, c)), full((2, c)), full((2, ICLR_LORA, c)),
                  full((1, c)), full((1, c))],
        out_specs=pl.BlockSpec((1, 2, 1, SCAN_ROWS, SCAN_K), lambda b, i: (b, 0, i, 0, 0)),
        out_shape=jax.ShapeDtypeStruct((nb, 2, s // tb, SCAN_ROWS, SCAN_K), BF16),
        compiler_params=_cp(("parallel", "parallel")),
        name="rwkv_prep",
    )(rkv, rkv, lora, w0, wup, a0, aup, k_k.reshape(1, c), k_a.reshape(1, c))


def scan_selectors():
    tb, grp = SCAN_TB, SCAN_GROUP
    src = jnp.arange(SCAN_K)
    term, head, time = src // (2 * tb), (src // tb) % 2, src % tb
    dst = jnp.arange(SCAN_NG * LANES)
    gather = ((time // grp)[:, None] == (dst // LANES)[None, :]) & (
        (term * 2 * grp + head * grp + time % grp)[:, None] == (dst % LANES)[None, :])
    j = jnp.arange(LANES)
    j_valid, j_head, j_step = j < 3 * 2 * grp, (j // grp) % 2, j % grp
    col = jnp.arange(grp * LANES)
    col_step, col_head = col // LANES, (col % LANES) // RWKV_HEAD
    base = j_valid[:, None] & (j_head[:, None] == col_head[None, :])
    expand = jnp.stack([base & (j_step[:, None] == col_step[None, :]),
                        base & (j_step[:, None] == grp - 1 - col_step[None, :])])
    lane_head = jnp.arange(LANES) // RWKV_HEAD
    base_l = j_valid[:, None] & (j_head[:, None] == lane_head[None, :])
    last = jnp.stack([base_l & (j_step[:, None] == grp - 1), base_l & (j_step[:, None] == 0)])
    return gather.astype(BF16), expand.astype(BF16), last.astype(BF16)


def _scan_kernel(xtf_ref, xtb_ref, vf_ref, vb_ref, gather_ref, expand_ref, last_ref, yf_ref, yb_ref,
                 t_scr, c_scr, big_scr, end_scr):
    tb, grp = SCAN_TB, SCAN_GROUP
    j = pl.program_id(1)

    @pl.when(j == 0)
    def _():
        t_scr[...] = jnp.zeros_like(t_scr)

    xts = (xtf_ref, xtb_ref)
    vs = (vf_ref, vb_ref)
    outs = (yf_ref, yb_ref)
    step_row = lax.broadcasted_iota(jnp.int32, (grp, LANES), 0)

    for d in range(2):
        packed = _dot(xts[d][0], gather_ref[...]).astype(BF16)
        for gi in range(SCAN_NG):
            c_scr[d, gi] = packed[:, gi * LANES:(gi + 1) * LANES]

    def group(g, carry):
        off = g * grp
        for d in range(2):
            lhs = c_scr[d, g if d == 0 else SCAN_NG - 1 - g]
            big_scr[d] = _dot(lhs[0:SCAN_STEP_ROWS], expand_ref[d])
            end_scr[d] = _dot(lhs[SCAN_STEP_ROWS:SCAN_ROWS], last_ref[d])
        for d in range(2):
            base = pl.multiple_of(off if d == 0 else tb - grp - off, grp)
            for p in range(SCAN_PAIRS):
                c = d * SCAN_PAIRS + p
                cols = pl.ds(p * LANES, LANES)
                v_win = vs[d][0, pl.ds(base, grp), cols]
                st = t_scr[c]
                y_win = jnp.zeros((grp, LANES), F32)
                for s in range(grp):
                    i = s if d == 0 else grp - 1 - s
                    col = lambda vec: big_scr[d, (vec * SCAN_PAIRS + p) * RWKV_HEAD:
                                              (vec * SCAN_PAIRS + p + 1) * RWKV_HEAD,
                                              s * LANES:(s + 1) * LANES]
                    s_kk = jnp.sum(st * col(0), axis=0, keepdims=True)
                    st = st - col(1) * s_kk + col(2) * v_win[i:i + 1, :]
                    y = jnp.sum(st * col(3), axis=0, keepdims=True)
                    y_win = jnp.where(step_row == i, y, y_win)
                t_scr[c] = st * end_scr[d, p * RWKV_HEAD:(p + 1) * RWKV_HEAD, :]
                outs[d][0, pl.ds(base, grp), cols] = y_win
        return carry

    lax.fori_loop(0, SCAN_NG, group, 0)


def rwkv_scan(xt, v_src):
    nb, _, nt, _, _ = xt.shape
    s = v_src.shape[1]
    c = RWKV_WIDTH
    tb = SCAN_TB
    nct = CTX_LEN // tb

    def bwd(j):
        return jnp.where(j < nct, nct - 1 - j, nt - 1 - (j - nct))

    gather, expand, last = scan_selectors()
    whole = pl.BlockSpec(memory_space=pltpu.VMEM)
    y_shape = jax.ShapeDtypeStruct((nb, s, c), F32)
    return pl.pallas_call(
        _scan_kernel,
        grid=(nb, nt),
        in_specs=[pl.BlockSpec((1, None, None, SCAN_ROWS, SCAN_K), lambda b, j: (b, 0, j, 0, 0)),
                  pl.BlockSpec((1, None, None, SCAN_ROWS, SCAN_K), lambda b, j: (b, 1, bwd(j), 0, 0)),
                  pl.BlockSpec((1, tb, c), lambda b, j: (b, j, 2)),
                  pl.BlockSpec((1, tb, c), lambda b, j: (b, bwd(j), 2)),
                  whole, whole, whole],
        out_specs=[pl.BlockSpec((1, tb, c), lambda b, j: (b, j, 0)),
                   pl.BlockSpec((1, tb, c), lambda b, j: (b, bwd(j), 0))],
        out_shape=[y_shape, y_shape],
        scratch_shapes=[pltpu.VMEM((2 * SCAN_PAIRS, RWKV_HEAD, LANES), F32),
                        pltpu.VMEM((2, SCAN_NG, SCAN_ROWS, LANES), BF16),
                        pltpu.VMEM((2, SCAN_STEP_ROWS, SCAN_GROUP * LANES), F32),
                        pltpu.VMEM((2, SCAN_PAIRS * RWKV_HEAD, LANES), F32)],
        compiler_params=_cp(("parallel", "arbitrary")),
        name="rwkv_scan",
    )(xt, xt, v_src, v_src, gather, expand, last)


def _readout_kernel(yf_ref, yb_ref, r_ref, k_ref, v_ref, gd_ref,
                    cb_ref, cc_ref, cx_ref, pc_ref, px_ref, nc_ref, nx_ref,
                    gup_ref, rk_ref, lw_ref, lb_ref, cw_ref, o_ref):
    i = pl.program_id(1)
    n_tiles = pl.num_programs(1)
    ones = _head_ones(RWKV_WIDTH, RWKV_HEAD)
    y = yf_ref[0] + yb_ref[0]
    mean = _dot_hi(y, ones) * (1.0 / RWKV_HEAD)
    yc = y - mean
    var = _dot_hi(yc * yc, ones) * (1.0 / RWKV_HEAD)
    out = yc * lax.rsqrt(var + LNX_EPS) * lw_ref[...] + lb_ref[...]
    bonus = _dot_hi(r_ref[0] * k_ref[0] * rk_ref[...], ones) * v_ref[0]
    gate = _dot_hi(jax.nn.sigmoid(gd_ref[0]), gup_ref[...])
    o_ref[0, :, 0:RWKV_WIDTH] = ((out + bonus) * gate).astype(BF16)

    u = cc_ref[0] * cx_ref[0]
    tm = u.shape[0]
    row = lax.broadcasted_iota(jnp.int32, u.shape, 0)
    has_prev = jnp.logical_and(i != 0, i != CTX_LEN // tm)
    has_next = jnp.logical_and(i != CTX_LEN // tm - 1, i != n_tiles - 1)
    u_prev_edge = jnp.where(has_prev, pc_ref[0, 7:8, :] * px_ref[0, 7:8, :], 0.0)
    u_next_edge = jnp.where(has_next, nc_ref[0, 0:1, :] * nx_ref[0, 0:1, :], 0.0)
    u_prev = jnp.where(row == 0, u_prev_edge, pltpu.roll(u, 1, 0))
    u_next = jnp.where(row == tm - 1, u_next_edge, pltpu.roll(u, tm - 1, 0))
    conv = u_prev * cw_ref[0:1, :] + u * cw_ref[1:2, :] + u_next * cw_ref[2:3, :]
    o_ref[0, :, RWKV_WIDTH:] = (cb_ref[0] * conv).astype(BF16)


def rwkv_readout(yf, yb, rkv, gd, conv, g_up, r_k, lnx_w, lnx_b, conv_w):
    nb, s, c = yf.shape
    tm = TOKEN_TILE
    n_tiles = s // tm
    per8 = tm // 8
    last8 = s // 8 - 1
    tok = lambda col, n=c: pl.BlockSpec((1, tm, n), lambda b, i: (b, i, col))
    prev = lambda col: pl.BlockSpec((1, 8, c), lambda b, i: (b, jnp.maximum(i * per8 - 1, 0), col))
    nxt = lambda col: pl.BlockSpec((1, 8, c), lambda b, i: (b, jnp.minimum((i + 1) * per8, last8), col))
    full = lambda shape: pl.BlockSpec(shape, lambda b, i: (0,) * len(shape))
    return pl.pallas_call(
        _readout_kernel,
        grid=(nb, n_tiles),
        in_specs=[tok(0), tok(0), tok(0), tok(1), tok(2), tok(0, 128),
                  tok(0), tok(1), tok(2), prev(1), prev(2), nxt(1), nxt(2),
                  full((GATE_LORA, c)), full((1, c)), full((1, c)), full((1, c)), full((3, c))],
        out_specs=pl.BlockSpec((1, tm, 2 * c), lambda b, i: (b, i, 0)),
        out_shape=jax.ShapeDtypeStruct((nb, s, 2 * c), BF16),
        compiler_params=_cp(("parallel", "parallel")),
        name="rwkv_readout",
    )(yf, yb, rkv, rkv, rkv, gd, conv, conv, conv, conv, conv, conv, conv,
      g_up, r_k.reshape(1, c), lnx_w.reshape(1, c), lnx_b.reshape(1, c), conv_w)


def _softmax_rows(s):
    m = jnp.max(s, axis=-1, keepdims=True)
    e = jnp.exp(s - m)
    return e / jnp.sum(e, axis=-1, keepdims=True)


def _diff_kernel(q_ref, k_ref, v_ref, lam_ref, sub_ref, o_ref, *, lam_init):
    i = pl.program_id(2)
    lam = lam_ref[...]
    lam_full = (jnp.exp(jnp.sum(lam[0:1] * lam[1:2], axis=1, keepdims=True))
                - jnp.exp(jnp.sum(lam[2:3] * lam[3:4], axis=1, keepdims=True)) + lam_init)
    q = q_ref[0] * (DIFF_HEAD ** -0.5)
    lane = lax.broadcasted_iota(jnp.int32, q.shape, 1)
    q1 = jnp.where(lane < DIFF_HEAD, q, jnp.zeros_like(q))
    q2 = jnp.where(lane < DIFF_HEAD, jnp.zeros_like(q), q)

    def attend(n_keys):
        k = k_ref[0, 0:n_keys, :]
        v = v_ref[0, 0:n_keys, :]

        def softmax_times_v(qm):
            s = _dot_nt(qm, k)
            e = jnp.exp(s - jnp.max(s, axis=-1, keepdims=True))
            return _dot(e.astype(BF16), v) / jnp.sum(e, axis=-1, keepdims=True)

        o = softmax_times_v(q1) - lam_full * softmax_times_v(q2)
        ms = jnp.mean(o * o, axis=-1, keepdims=True)
        o = o * lax.rsqrt(ms + SUBLN_EPS) * sub_ref[...] * (1.0 - lam_init)
        o_ref[0] = o.astype(BF16)

    @pl.when(i == 0)
    def _():
        attend(CTX_LEN)

    @pl.when(i != 0)
    def _():
        attend(k_ref.shape[1])


def diff_attention(dq, dk, dv, lam, subln, lam_init):
    nb, s, _ = dq.shape
    tm = TOKEN_TILE
    return pl.pallas_call(
        functools.partial(_diff_kernel, lam_init=lam_init),
        grid=(nb, DIFF_HEADS, s // tm),
        in_specs=[pl.BlockSpec((1, tm, LANES), lambda b, h, i: (b, i, h)),
                  pl.BlockSpec((1, s, LANES), lambda b, h, i: (b, 0, h)),
                  pl.BlockSpec((1, s, LANES), lambda b, h, i: (b, 0, h)),
                  pl.BlockSpec((4, DIFF_HEAD), lambda b, h, i: (0, 0)),
                  pl.BlockSpec((1, DIFF_VHEAD), lambda b, h, i: (0, 0))],
        out_specs=pl.BlockSpec((1, tm, LANES), lambda b, h, i: (b, i, h)),
        out_shape=jax.ShapeDtypeStruct((nb, s, DIFF_WIDTH), BF16),
        compiler_params=_cp(("parallel", "parallel", "parallel")),
        name="diff_attention",
    )(dq, dk, dv, lam, subln.reshape(1, DIFF_VHEAD))


def na_bias_table(rel_bias, rows):
    kr = min(NA_MAX_ROWS, rows)
    cols = jnp.arange(GRID_W)
    col_start = jnp.clip(cols - NA_COLS // 2, 0, GRID_W - NA_COLS)
    kc = jnp.arange(GRID_W)
    inside = (kc[None, :] >= col_start[:, None]) & (kc[None, :] < col_start[:, None] + NA_COLS)
    col_off = kc[None, :] - cols[:, None] + (NA_COLS - 1)
    row_off = jnp.arange(kr)[None, :] - jnp.arange(NA_MAX_ROWS)[:, None] + (NA_MAX_ROWS - 1)
    pick_row = (row_off[:, :, None] == jnp.arange(2 * NA_MAX_ROWS - 1)).astype(F32)
    pick_col = ((col_off[:, :, None] == jnp.arange(2 * NA_COLS - 1)) & inside[:, :, None]).astype(F32)
    b = jnp.einsum('hro,pjr,cko->hpcjk', rel_bias.astype(F32), pick_row, pick_col, precision=HI)
    b = jnp.where(inside[None, None, :, None, :], b, NEG_BIG)
    return b.reshape(rel_bias.shape[0], NA_MAX_ROWS, GRID_W, kr * GRID_W)


def _na_kernel(q_ref, k_ref, v_ref, bias_ref, o_ref, *, rows):
    kr = min(NA_MAX_ROWS, rows)
    scale = NA_HEAD ** -0.5
    w = GRID_W
    lane_q = lax.broadcasted_iota(jnp.int32, (2 * w, LANES), 1)
    row_q = lax.broadcasted_iota(jnp.int32, (2 * w, LANES), 0)
    own = (lane_q < NA_HEAD) == (row_q < w)
    first_head = lax.broadcasted_iota(jnp.int32, (w, LANES), 1) < NA_HEAD
    k_ctx = k_ref[0, 0:CTX_LEN, :]
    v_ctx = v_ref[0, 0:CTX_LEN, :]

    def two_heads(q):
        q2 = jnp.concatenate([q, q], axis=0)
        return jnp.where(own, q2, jnp.zeros_like(q2))

    def merge(o):
        return jnp.where(first_head, o[0:w], o[w:2 * w])

    def body(r, carry):
        r_start = jnp.clip(r - kr // 2, 0, rows - kr)
        q = two_heads(q_ref[0, pl.ds(pl.multiple_of(CTX_LEN + r * w, w), w), :])
        base = pl.multiple_of(CTX_LEN + r_start * w, w)
        k_win = k_ref[0, pl.ds(base, kr * w), :]
        v_win = v_ref[0, pl.ds(base, kr * w), :]
        place = r - r_start
        bias = jnp.concatenate([bias_ref[0, place], bias_ref[1, place]], axis=0)
        s_loc = _dot_nt(q, k_win) * scale + bias
        s_ctx = _dot_nt(q, k_ctx) * scale
        m = jnp.maximum(jnp.max(s_loc, axis=-1, keepdims=True), jnp.max(s_ctx, axis=-1, keepdims=True))
        e_loc = jnp.exp(s_loc - m)
        e_ctx = jnp.exp(s_ctx - m)
        z = jnp.sum(e_loc, axis=-1, keepdims=True) + jnp.sum(e_ctx, axis=-1, keepdims=True)
        o = (_dot((e_ctx / z).astype(BF16), v_ctx) + _dot((e_loc / z).astype(BF16), v_win))
        o_ref[0, pl.ds(pl.multiple_of(CTX_LEN + r * w, w), w), :] = merge(o).astype(BF16)
        return carry

    lax.fori_loop(0, rows, body, 0, unroll=2)

    for blk in range(CTX_LEN // w):
        q = two_heads(q_ref[0, blk * w:(blk + 1) * w, :])
        p = _softmax_rows(_dot_nt(q, k_ctx) * scale)
        o_ref[0, blk * w:(blk + 1) * w, :] = merge(_dot(p.astype(BF16), v_ctx)).astype(BF16)


def na_attention(nq, nk, nv, bias_tab, rows):
    nb, s, _ = nq.shape
    kr = min(NA_MAX_ROWS, rows)
    seq = pl.BlockSpec((1, s, LANES), lambda b, p: (b, 0, p))
    return pl.pallas_call(
        functools.partial(_na_kernel, rows=rows),
        grid=(nb, NA_HEADS // 2),
        in_specs=[seq, seq, seq,
                  pl.BlockSpec((2, NA_MAX_ROWS, GRID_W, kr * GRID_W), lambda b, p: (p, 0, 0, 0))],
        out_specs=seq,
        out_shape=jax.ShapeDtypeStruct((nb, s, NA_WIDTH), BF16),
        compiler_params=_cp(("parallel", "parallel")),
        name="na_attention",
    )(nq, nk, nv, bias_tab)


def _route(logits):
    lane = lax.broadcasted_iota(jnp.int32, logits.shape, 1)
    big = jnp.int32(LANES)
    is_g = jnp.logical_and(lane >= N_EXPERTS, lane < N_EXPERTS + N_GROUPS)
    lg = jnp.where(is_g, logits, NEG_BIG)
    mg = jnp.max(lg, axis=-1, keepdims=True)
    zg = jnp.sum(jnp.where(is_g, jnp.exp(lg - mg), 0.0), axis=-1, keepdims=True)
    pg_top = 1.0 / zg
    g_sel = jnp.min(jnp.where(jnp.logical_and(is_g, lg == mg), lane, big), axis=-1, keepdims=True) - N_EXPERTS
    in_g = jnp.logical_and(lane < N_EXPERTS, (lane // EXPERTS_PER_GROUP) == g_sel)
    le = jnp.where(in_g, logits, NEG_BIG)
    m1 = jnp.max(le, axis=-1, keepdims=True)
    i1 = jnp.min(jnp.where(jnp.logical_and(in_g, le == m1), lane, big), axis=-1, keepdims=True)
    rest = jnp.logical_and(in_g, lane != i1)
    le2 = jnp.where(rest, logits, NEG_BIG)
    m2 = jnp.max(le2, axis=-1, keepdims=True)
    i2 = jnp.min(jnp.where(jnp.logical_and(rest, le2 == m2), lane, big), axis=-1, keepdims=True)
    e2 = jnp.exp(m2 - m1)
    w1 = pg_top / (1.0 + e2)
    w2 = pg_top * e2 / (1.0 + e2)
    return (jnp.where(lane == i1, w1, 0.0) + jnp.where(lane == i2, w2, 0.0)
            + jnp.where(lane == N_EXPERTS, g_sel.astype(F32), 0.0))


def _outproj_kernel(x_ref, oa_ref, ob_ref, mod_ref, g_ref, w_ref, wr_ref, br_ref,
                    xo_ref, hx_ref):
    m = mod_ref[0]
    half = oa_ref.shape[2]
    d = x_ref.shape[2]
    y = _dot(oa_ref[0], w_ref[0:half, :]) + _dot(ob_ref[0], w_ref[half:, :])
    x = x_ref[0] + m[2:3] * y
    xo_ref[0] = x
    h = _norm_mod(x, g_ref[...], m[3:4], m[4:5])
    hx_ref[0, :, 0:d] = h
    hx_ref[0, :, d:] = _route(_dot_hi(h, wr_ref[...]) + br_ref[...])


def outproj_route(x, oa, ob, ob_col, modl, g, w_out, w_route, b_route):
    nb, s, d = x.shape
    tm = TOKEN_TILE
    half = d // 2
    tok = lambda n, col=0: pl.BlockSpec((1, tm, n), lambda b, i: (b, i, col))
    full = lambda shape: pl.BlockSpec(shape, lambda b, i: (0,) * len(shape))
    return pl.pallas_call(
        _outproj_kernel,
        grid=(nb, s // tm),
        in_specs=[tok(d), tok(half), tok(half, ob_col),
                  pl.BlockSpec((1, 8, d), _mod_index(nb)),
                  full((1, d)), full((d, d)), full((d, LANES)), full((1, LANES))],
        out_specs=[tok(d), tok(d + LANES)],
        out_shape=[jax.ShapeDtypeStruct((nb, s, d), F32),
                   jax.ShapeDtypeStruct((nb, s, d + LANES), F32)],
        compiler_params=_cp(("parallel", "parallel")),
        name="outproj_route",
    )(x, oa, ob, modl, g, w_out, w_route, b_route)


def moe_group_tables(gid):
    n_tok = gid.shape[0]
    n_rows = n_tok + N_GROUPS * MOE_ROWS
    onehot = (gid[:, None] == jnp.arange(N_GROUPS)[None, :]).astype(jnp.int32)
    rank = jnp.sum((jnp.cumsum(onehot, axis=0) - onehot) * onehot, axis=1)
    count = jnp.sum(onehot, axis=0)
    padded = (count + MOE_ROWS - 1) // MOE_ROWS * MOE_ROWS
    end = jnp.cumsum(padded)
    pos = (end - padded)[gid] + rank
    src = jnp.zeros((n_rows,), jnp.int32).at[pos].set(jnp.arange(n_tok, dtype=jnp.int32))
    real = jnp.zeros((n_rows,), jnp.bool_).at[pos].set(True)
    spare = n_tok + jnp.cumsum((~real).astype(jnp.int32)) - 1
    dst = jnp.where(real, src, spare).astype(jnp.int32)
    tile_start = jnp.arange(n_rows // MOE_ROWS) * MOE_ROWS
    tile_group = jnp.minimum(jnp.sum(tile_start[:, None] >= end[None, :], axis=1), N_GROUPS - 1)
    return tile_group.astype(jnp.int32), src, dst


def _moe_group_kernel(grp_ref, src_ref, dst_ref, hx_hbm, w1_ref, w3_ref, w2_ref, y_hbm,
                      hbuf, ybuf, in_sem, out_sem):
    i = pl.program_id(0)
    n = pl.num_programs(0)
    slot = lax.rem(i, 2)
    d = w1_ref.shape[1]

    def row_in(tile, sl, r):
        return pltpu.make_async_copy(hx_hbm.at[pl.ds(src_ref[tile * MOE_ROWS + r], 1)],
                                     hbuf.at[sl, pl.ds(r, 1)], in_sem.at[sl])

    def row_out(tile, sl, r):
        return pltpu.make_async_copy(ybuf.at[sl, pl.ds(r, 1)],
                                     y_hbm.at[pl.ds(dst_ref[tile * MOE_ROWS + r], 1)], out_sem.at[sl])

    def start_all(make, tile, sl):
        for r in range(MOE_ROWS):
            make(tile, sl, r).start()

    def wait_in(sl):
        pltpu.make_async_copy(hx_hbm.at[pl.ds(0, MOE_ROWS)], hbuf.at[sl], in_sem.at[sl]).wait()

    def wait_out(sl):
        pltpu.make_async_copy(ybuf.at[sl], y_hbm.at[pl.ds(0, MOE_ROWS)], out_sem.at[sl]).wait()

    @pl.when(i == 0)
    def _():
        start_all(row_in, 0, 0)

    start_all(row_in, jnp.minimum(i + 1, n - 1), 1 - slot)
    wait_in(slot)

    @pl.when(i >= 2)
    def _():
        wait_out(slot)

    rows = hbuf[slot]
    h = rows[:, 0:d].astype(BF16)
    comb = rows[:, d:]
    lane = lax.broadcasted_iota(jnp.int32, comb.shape, 1)
    first_expert = grp_ref[i] * EXPERTS_PER_GROUP
    y = jnp.zeros((MOE_ROWS, d), F32)
    for j in range(EXPERTS_PER_GROUP):
        a = _dot(h, w1_ref[j])
        a = a * jax.nn.sigmoid(a) * _dot(h, w3_ref[j])
        gate = jnp.sum(jnp.where(lane == first_expert + j, comb, 0.0), axis=-1, keepdims=True)
        y = y + gate * _dot(a.astype(BF16), w2_ref[j])
    ybuf[slot] = y
    start_all(row_out, i, slot)

    @pl.when(i == n - 1)
    def _():
        wait_in(1 - slot)
        wait_out(1 - slot)
        wait_out(slot)


def moe_grouped(hx, w1, w3, w2):
    n_tok, width = hx.shape
    d = width - LANES
    tile_group, src, dst = moe_group_tables(hx[:, d + N_EXPERTS].astype(jnp.int32))
    n_rows = src.shape[0]
    n_tiles = n_rows // MOE_ROWS
    assert n_tiles >= 2
    grouped = lambda arr: arr.reshape((N_GROUPS, EXPERTS_PER_GROUP) + arr.shape[1:])
    wspec = lambda a, b: pl.BlockSpec((None, EXPERTS_PER_GROUP, a, b), lambda i, grp, s_, d_: (grp[i], 0, 0, 0))
    return pl.pallas_call(
        _moe_group_kernel,
        grid_spec=pltpu.PrefetchScalarGridSpec(
            num_scalar_prefetch=3,
            grid=(n_tiles,),
            in_specs=[pl.BlockSpec(memory_space=pl.ANY),
                      wspec(d, D_EXPERT), wspec(d, D_EXPERT), wspec(D_EXPERT, d)],
            out_specs=pl.BlockSpec(memory_space=pl.ANY),
            scratch_shapes=[pltpu.VMEM((2, MOE_ROWS, width), F32),
                            pltpu.VMEM((2, MOE_ROWS, d), F32),
                            pltpu.SemaphoreType.DMA((2,)),
                            pltpu.SemaphoreType.DMA((2,))]),
        out_shape=jax.ShapeDtypeStruct((n_rows, d), F32),
        compiler_params=_cp(("arbitrary",)),
        name="moe_grouped",
    )(tile_group, src, dst, hx, grouped(w1), grouped(w3), grouped(w2))


def _moe_residual_kernel(x_ref, y_ref, mod_ref, o_ref):
    o_ref[0] = x_ref[0] + mod_ref[0][5:6] * y_ref[...]


def moe_residual(x, y, modl):
    nb, s, d = x.shape
    tm = TOKEN_TILE
    per_row = s // tm
    tok = pl.BlockSpec((1, tm, d), lambda b, i: (b, i, 0))
    return pl.pallas_call(
        _moe_residual_kernel,
        grid=(nb, per_row),
        in_specs=[tok,
                  pl.BlockSpec((tm, d), lambda b, i: (b * per_row + i, 0)),
                  pl.BlockSpec((1, 8, d), _mod_index(nb))],
        out_specs=tok,
        out_shape=jax.ShapeDtypeStruct((nb, s, d), F32),
        compiler_params=_cp(("parallel", "parallel")),
        name="moe_residual",
    )(x, y, modl)


def _final_kernel(x_ref, g_ref, o_ref):
    x = x_ref[0]
    ms = jnp.mean(x * x, axis=-1, keepdims=True)
    o_ref[0] = x * lax.rsqrt(ms + RMS_EPS) * g_ref[...]


def final_norm(x, g):
    nb, s, d = x.shape
    tm = TOKEN_TILE
    skip = CTX_LEN // tm
    return pl.pallas_call(
        _final_kernel,
        grid=(nb, (s - CTX_LEN) // tm),
        in_specs=[pl.BlockSpec((1, tm, d), lambda b, i: (b, i + skip, 0)),
                  pl.BlockSpec((1, d), lambda b, i: (0, 0))],
        out_specs=pl.BlockSpec((1, tm, d), lambda b, i: (b, i, 0)),
        out_shape=jax.ShapeDtypeStruct((nb, s - CTX_LEN, d), F32),
        compiler_params=_cp(("parallel", "parallel")),
        name="final_norm",
    )(x, g.reshape(1, d))


def kernel(x, c, ctx, c_ctx, ada_w, ada_b, norm_g, final_g, ev_w_in, ev_w_out, ev_decay_w0, ev_decay_up, ev_iclr_a0, ev_iclr_up, ev_gate_up, ev_k_k, ev_k_a, ev_r_k, ev_lnx_w, ev_lnx_b, ev_conv_w, od_w_in, od_w_out, od_lambda, od_subln, od_rel_bias, moe_wg, moe_bg, moe_we, moe_be, moe_w1, moe_w3, moe_w2):
    nb, seq, d = x.shape
    rows = seq // GRID_W
    depth = ada_w.shape[0]
    assert nb <= 15 and seq % TOKEN_TILE == 0 and ctx.shape[1] == CTX_LEN

    xs = jnp.concatenate([ctx, x], axis=1)
    cond = jnp.zeros((16, d), F32).at[:nb].set(c).at[nb].set(c_ctx)
    mod = ada_all(cond, ada_w, ada_b)
    mod = mod[:, :nb + 1].reshape(depth, nb + 1, 6, d)
    mod = jnp.concatenate([mod, jnp.zeros((depth, nb + 1, 2, d), F32)], axis=2)
    cos_t, sin_t = rope_tables(seq)

    for l in range(depth):
        i = l // 2
        modl = mod[l]
        g1 = norm_g[l, 0].reshape(1, d)
        g2 = norm_g[l, 1].reshape(1, d)
        if l % 2 == 0:
            rkv, lora, gd, conv = inproj_even(xs, modl, g1, ev_w_in[i].astype(BF16))
            xt = rwkv_prep(rkv, lora, ev_decay_w0[i], ev_decay_up[i], ev_iclr_a0[i],
                           ev_iclr_up[i], ev_k_k[i], ev_k_a[i])
            yf, yb = rwkv_scan(xt, rkv)
            o = rwkv_readout(yf, yb, rkv, gd, conv, ev_gate_up[i], ev_r_k[i], ev_lnx_w[i],
                             ev_lnx_b[i], ev_conv_w[i])
            oa, ob, ob_col = o, o, 1
            w_out = ev_w_out[i]
        else:
            dq, dk, dv, nq, nk, nv = inproj_odd(xs, modl, g1, od_w_in[i].astype(BF16), cos_t, sin_t)
            lam_init = 0.8 - 0.6 * math.exp(-0.3 * l)
            oa = diff_attention(dq, dk, dv, od_lambda[i], od_subln[i], lam_init)
            ob = na_attention(nq, nk, nv, na_bias_table(od_rel_bias[i], rows), rows)
            ob_col = 0
            w_out = od_w_out[i]
        w_route = jnp.concatenate(
            [moe_we[l], moe_wg[l], jnp.zeros((d, LANES - N_EXPERTS - N_GROUPS), F32)], axis=1)
        b_route = jnp.concatenate(
            [moe_be[l], moe_bg[l], jnp.zeros((LANES - N_EXPERTS - N_GROUPS,), F32)]).reshape(1, LANES)
        xs, hx = outproj_route(xs, oa, ob, ob_col, modl, g2, w_out.astype(BF16), w_route, b_route)
        y = moe_grouped(hx.reshape(nb * xs.shape[1], d + LANES), moe_w1[l].astype(BF16),
                        moe_w3[l].astype(BF16), moe_w2[l].astype(BF16))
        xs = moe_residual(xs, y, modl)
    return final_norm(xs, final_g)
```

```python
import functools
import math

import jax
import jax.numpy as jnp
from jax import lax
from jax.experimental import pallas as pl
from jax.experimental.pallas import tpu as pltpu

F32 = jnp.float32
BF16 = jnp.bfloat16
HI = lax.Precision.HIGHEST

D_MODEL = 1024
DEPTH = 4
GRID_W = 64
CTX_LEN = 256

RWKV_WIDTH = 512
RWKV_HEAD = 64
RWKV_HEADS = 8
DECAY_LORA = 64
ICLR_LORA = 64
GATE_LORA = 128
CONV_WIDTH = 512
LNX_EPS = 64e-5

DIFF_HEADS = 4
DIFF_HEAD = 64
DIFF_VHEAD = 128
DIFF_WIDTH = 512
NA_HEADS = 8
NA_HEAD = 64
NA_WIDTH = 512
NA_MAX_ROWS = 8
NA_COLS = 16
ROPE_BASE = 10000.0
SUBLN_EPS = 1e-5

N_GROUPS = 4
EXPERTS_PER_GROUP = 4
N_EXPERTS = 16
D_EXPERT = 512
RMS_EPS = 1e-6

EVEN_PROJ = 3456
ODD_PROJ = 3072

LANES = 128
TOKEN_TILE = 256
MOE_ROWS = 256
NA_ROWS_PER_ITER = 4
SCAN_TB = 128
SCAN_GROUP = 16
SCAN_VECS = 5
SCAN_PAIRS = RWKV_HEADS // 2
SCAN_ROWS = SCAN_VECS * SCAN_PAIRS * RWKV_HEAD
SCAN_STEP_ROWS = (SCAN_VECS - 1) * SCAN_PAIRS * RWKV_HEAD
SCAN_K = 3 * 2 * SCAN_TB
SCAN_NG = SCAN_TB // SCAN_GROUP
VMEM_LIMIT = 56 * 1024 * 1024
NEG_BIG = -1e30


def _cp(sem, vmem=VMEM_LIMIT):
    return pltpu.CompilerParams(dimension_semantics=sem, vmem_limit_bytes=vmem)


def _dot(a, b):
    return jnp.dot(a, b, preferred_element_type=F32)


def _dot_hi(a, b):
    return jnp.dot(a, b, preferred_element_type=F32, precision=HI)


def _dot_nt(a, b):
    return lax.dot_general(a, b, (((1,), (1,)), ((), ())), preferred_element_type=F32)


def _split3(x):
    hi = x.astype(BF16)
    r1 = x - hi.astype(F32)
    mid = r1.astype(BF16)
    lo = (r1 - mid.astype(F32)).astype(BF16)
    return hi, mid, lo


def _head_ones(width, head):
    i = (lax.broadcasted_iota(jnp.int32, (3 * width, width), 0) % width) // head
    j = lax.broadcasted_iota(jnp.int32, (3 * width, width), 1) // head
    return (i == j).astype(BF16)


def _head_sum(x, ones3):
    return _dot(jnp.concatenate(_split3(x), axis=1), ones3)


def _ada_kernel(cond_ref, w_ref, b_ref, o_ref):
    cond = cond_ref[...]
    s = cond * jax.nn.sigmoid(cond)
    o_ref[0] = _dot_hi(s, w_ref[0]) + b_ref[0]


def ada_all(cond, ada_w, ada_b):
    depth, d, n = ada_w.shape
    tn = 1536
    return pl.pallas_call(
        _ada_kernel,
        grid=(depth, n // tn),
        in_specs=[pl.BlockSpec((16, d), lambda l, j: (0, 0)),
                  pl.BlockSpec((1, d, tn), lambda l, j: (l, 0, j)),
                  pl.BlockSpec((1, 1, tn), lambda l, j: (l, 0, j))],
        out_specs=pl.BlockSpec((1, 16, tn), lambda l, j: (l, 0, j)),
        out_shape=jax.ShapeDtypeStruct((depth, 16, n), F32),
        compiler_params=_cp(("parallel", "parallel")),
        name="ada_mod",
    )(cond, ada_w, ada_b.reshape(depth, 1, n))


def _mod_index(n_batch):
    return lambda b, i: (jnp.where(i == 0, n_batch, b), 0, 0)


def _norm_mod(x, g, shift, scale):
    ms = jnp.mean(x * x, axis=-1, keepdims=True)
    return x * lax.rsqrt(ms + RMS_EPS) * g * (1.0 + scale) + shift


def _inproj_even_kernel(x_ref, mod_ref, g_ref, w_ref, rkv_ref, lora_ref, gd_ref, conv_ref):
    m = mod_ref[0]
    h = _norm_mod(x_ref[0], g_ref[...], m[0:1], m[1:2]).astype(BF16)
    p = _dot(h, w_ref[...])
    rkv_ref[0] = p[:, 0:1536]
    lora_ref[0] = p[:, 1536:1792]
    gd_ref[0] = p[:, 1792:1920]
    conv_ref[0] = p[:, 1920:3456]


def inproj_even(x, modl, g, w):
    nb, s, d = x.shape
    tm = TOKEN_TILE
    tok = lambda n: pl.BlockSpec((1, tm, n), lambda b, i: (b, i, 0))
    return pl.pallas_call(
        _inproj_even_kernel,
        grid=(nb, s // tm),
        in_specs=[tok(d),
                  pl.BlockSpec((1, 8, d), _mod_index(nb)),
                  pl.BlockSpec((1, d), lambda b, i: (0, 0)),
                  pl.BlockSpec((d, EVEN_PROJ), lambda b, i: (0, 0))],
        out_specs=[tok(1536), tok(256), tok(128), tok(1536)],
        out_shape=[jax.ShapeDtypeStruct((nb, s, 1536), F32),
                   jax.ShapeDtypeStruct((nb, s, 256), F32),
                   jax.ShapeDtypeStruct((nb, s, 128), F32),
                   jax.ShapeDtypeStruct((nb, s, 1536), F32)],
        compiler_params=_cp(("parallel", "parallel")),
        name="inproj_even",
    )(x, modl, g, w)


def _rope(x, cos, sin_signed):
    lane = lax.broadcasted_iota(jnp.int32, x.shape, 1)
    partner = jnp.where((lane % 32) < 16, pltpu.roll(x, LANES - 16, 1), pltpu.roll(x, 16, 1))
    return x * cos + partner * sin_signed


def _inproj_odd_kernel(x_ref, mod_ref, g_ref, w_ref, cos_ref, sin_ref,
                       dq_ref, dk_ref, dv_ref, nq_ref, nk_ref, nv_ref):
    m = mod_ref[0]
    h = _norm_mod(x_ref[0], g_ref[...], m[0:1], m[1:2]).astype(BF16)
    p = _dot(h, w_ref[...])
    cos = cos_ref[...]
    sin = sin_ref[...]
    for out_ref, base in ((dq_ref, 0), (dk_ref, 512)):
        for c in range(4):
            blk = p[:, base + c * LANES: base + (c + 1) * LANES]
            out_ref[0, :, c * LANES:(c + 1) * LANES] = _rope(blk, cos, sin).astype(BF16)
    dv_ref[0] = p[:, 1024:1536].astype(BF16)
    nq_ref[0] = p[:, 1536:2048].astype(BF16)
    nk_ref[0] = p[:, 2048:2560].astype(BF16)
    nv_ref[0] = p[:, 2560:3072].astype(BF16)


def inproj_odd(x, modl, g, w, cos_t, sin_t):
    nb, s, d = x.shape
    tm = TOKEN_TILE
    tok = lambda n: pl.BlockSpec((1, tm, n), lambda b, i: (b, i, 0))
    tab = pl.BlockSpec((tm, LANES), lambda b, i: (i, 0))
    return pl.pallas_call(
        _inproj_odd_kernel,
        grid=(nb, s // tm),
        in_specs=[tok(d),
                  pl.BlockSpec((1, 8, d), _mod_index(nb)),
                  pl.BlockSpec((1, d), lambda b, i: (0, 0)),
                  pl.BlockSpec((d, ODD_PROJ), lambda b, i: (0, 0)),
                  tab, tab],
        out_specs=[tok(512)] * 6,
        out_shape=[jax.ShapeDtypeStruct((nb, s, 512), BF16)] * 6,
        compiler_params=_cp(("parallel", "parallel")),
        name="inproj_odd",
    )(x, modl, g, w, cos_t, sin_t)


def rope_tables(seq):
    n_freq = DIFF_HEAD // 4
    inv_freq = ROPE_BASE ** (-jnp.arange(n_freq, dtype=F32) / n_freq)
    t = jnp.arange(seq, dtype=jnp.int32)
    pos = jnp.stack([t // GRID_W, t % GRID_W], axis=-1).astype(F32)
    ang = pos[:, :, None] * inv_freq
    cos, sin = jnp.cos(ang), jnp.sin(ang)
    cos64 = jnp.concatenate([cos[:, 0], cos[:, 0], cos[:, 1], cos[:, 1]], axis=-1)
    sin64 = jnp.concatenate([-sin[:, 0], sin[:, 0], -sin[:, 1], sin[:, 1]], axis=-1)
    cos_t = jnp.concatenate([jnp.ones((CTX_LEN, 64), F32), cos64], axis=0)
    sin_t = jnp.concatenate([jnp.zeros((CTX_LEN, 64), F32), sin64], axis=0)
    return jnp.tile(cos_t, (1, 2)), jnp.tile(sin_t, (1, 2))


def _prep_kernel(r_ref, k_ref, lora_ref, w0_ref, wup_ref, a0_ref, aup_ref, kk_w_ref, ka_ref, xt_ref):
    k = k_ref[0]
    r = r_ref[0]
    lora = lora_ref[0]
    tb = SCAN_TB
    ones = _head_ones(RWKV_WIDTH, RWKV_HEAD)
    kk = k * kk_w_ref[...]
    ss = _head_sum(kk * kk, ones)
    kk = kk * lax.rsqrt(jnp.maximum(ss, 1e-24))
    ti = lax.broadcasted_iota(jnp.int32, (tb, tb), 0)
    tj = lax.broadcasted_iota(jnp.int32, (tb, tb), 1)
    same_group = (ti // SCAN_GROUP) == (tj // SCAN_GROUP)
    upto = [jnp.tile(jnp.logical_and(same_group, cmp).astype(BF16), (1, 3))
            for cmp in (tj <= ti,
                        tj >= ti)]

    def put(d, vec, x):
        for term_i, term in enumerate(_split3(x.T)):
            for h in range(RWKV_HEADS):
                row0 = (vec * SCAN_PAIRS + h // 2) * RWKV_HEAD
                lane0 = (term_i * 2 + h % 2) * SCAN_TB
                xt_ref[0, d, 0, row0:row0 + RWKV_HEAD, lane0:lane0 + SCAN_TB] = (
                    term[h * RWKV_HEAD:(h + 1) * RWKV_HEAD, :])

    for d in range(2):
        wd = lora[:, d * 64:(d + 1) * 64]
        ad = lora[:, 128 + d * 64:128 + (d + 1) * 64]
        w = w0_ref[d:d + 1, :] + _dot_hi(jnp.tanh(wd), wup_ref[d])
        log_decay = -math.exp(-0.5) * jax.nn.sigmoid(w)
        a = jax.nn.sigmoid(a0_ref[d:d + 1, :] + _dot_hi(ad, aup_ref[d]))
        log_g = _dot(upto[d], jnp.concatenate(_split3(log_decay), axis=0))
        g = jnp.exp(log_g)
        g_inv = jnp.exp(-log_g)
        put(d, 0, kk * jnp.exp(log_g - log_decay))
        put(d, 1, kk * a * g_inv)
        put(d, 2, k * (1.0 + (a - 1.0) * ka_ref[...]) * g_inv)
        put(d, 3, r * g)
        put(d, 4, g)


def rwkv_prep(rkv, lora, w0, wup, a0, aup, k_k, k_a):
    nb, s, _ = rkv.shape
    tb = SCAN_TB
    c = RWKV_WIDTH
    full = lambda shape: pl.BlockSpec(shape, lambda b, i: (0,) * len(shape))
    return pl.pallas_call(
        _prep_kernel,
        grid=(nb, s // tb),
        in_specs=[pl.BlockSpec((1, tb, c), lambda b, i: (b, i, 0)),
                  pl.BlockSpec((1, tb, c), lambda b, i: (b, i, 1)),
                  pl.BlockSpec((1, tb, 256), lambda b, i: (b, i, 0)),
                  full((2, c)), full((2, DECAY_LORA, c)), full((2, c)), full((2, ICLR_LORA, c)),
                  full((1, c)), full((1, c))],
        out_specs=pl.BlockSpec((1, 2, 1, SCAN_ROWS, SCAN_K), lambda b, i: (b, 0, i, 0, 0)),
        out_shape=jax.ShapeDtypeStruct((nb, 2, s // tb, SCAN_ROWS, SCAN_K), BF16),
        compiler_params=_cp(("parallel", "parallel")),
        name="rwkv_prep",
    )(rkv, rkv, lora, w0, wup, a0, aup, k_k.reshape(1, c), k_a.reshape(1, c))


def scan_selectors():
    tb, grp = SCAN_TB, SCAN_GROUP
    src = jnp.arange(SCAN_K)
    term, head, time = src // (2 * tb), (src // tb) % 2, src % tb
    dst = jnp.arange(SCAN_NG * LANES)
    gather = ((time // grp)[:, None] == (dst // LANES)[None, :]) & (
        (term * 2 * grp + head * grp + time % grp)[:, None] == (dst % LANES)[None, :])
    j = jnp.arange(LANES)
    j_valid, j_head, j_step = j < 3 * 2 * grp, (j // grp) % 2, j % grp
    col = jnp.arange(grp * LANES)
    col_step, col_head = col // LANES, (col % LANES) // RWKV_HEAD
    base = j_valid[:, None] & (j_head[:, None] == col_head[None, :])
    expand = jnp.stack([base & (j_step[:, None] == col_step[None, :]),
                        base & (j_step[:, None] == grp - 1 - col_step[None, :])])
    lane_head = jnp.arange(LANES) // RWKV_HEAD
    base_l = j_valid[:, None] & (j_head[:, None] == lane_head[None, :])
    last = jnp.stack([base_l & (j_step[:, None] == grp - 1), base_l & (j_step[:, None] == 0)])
    return gather.astype(BF16), expand.astype(BF16), last.astype(BF16)


def _scan_kernel(xtf_ref, xtb_ref, vf_ref, vb_ref, gather_ref, expand_ref, last_ref, yf_ref, yb_ref,
                 t_scr, c_scr, big_scr, end_scr):
    tb, grp = SCAN_TB, SCAN_GROUP
    j = pl.program_id(1)

    @pl.when(j == 0)
    def _():
        t_scr[...] = jnp.zeros_like(t_scr)

    xts = (xtf_ref, xtb_ref)
    vs = (vf_ref, vb_ref)
    outs = (yf_ref, yb_ref)
    step_row = lax.broadcasted_iota(jnp.int32, (grp, LANES), 0)

    for d in range(2):
        packed = _dot(xts[d][0], gather_ref[...]).astype(BF16)
        for gi in range(SCAN_NG):
            c_scr[d, gi] = packed[:, gi * LANES:(gi + 1) * LANES]

    def group(g, carry):
        off = g * grp
        for d in range(2):
            lhs = c_scr[d, g if d == 0 else SCAN_NG - 1 - g]
            big_scr[d] = _dot(lhs[0:SCAN_STEP_ROWS], expand_ref[d])
            end_scr[d] = _dot(lhs[SCAN_STEP_ROWS:SCAN_ROWS], last_ref[d])
        for d in range(2):
            base = pl.multiple_of(off if d == 0 else tb - grp - off, grp)
            for p in range(SCAN_PAIRS):
                c = d * SCAN_PAIRS + p
                cols = pl.ds(p * LANES, LANES)
                v_win = vs[d][0, pl.ds(base, grp), cols]
                st = t_scr[c]
                y_win = jnp.zeros((grp, LANES), F32)
                for s in range(grp):
                    i = s if d == 0 else grp - 1 - s
                    col = lambda vec: big_scr[d, (vec * SCAN_PAIRS + p) * RWKV_HEAD:
                                              (vec * SCAN_PAIRS + p + 1) * RWKV_HEAD,
                                              s * LANES:(s + 1) * LANES]
                    s_kk = jnp.sum(st * col(0), axis=0, keepdims=True)
                    st = st - col(1) * s_kk + col(2) * v_win[i:i + 1, :]
                    y = jnp.sum(st * col(3), axis=0, keepdims=True)
                    y_win = jnp.where(step_row == i, y, y_win)
                t_scr[c] = st * end_scr[d, p * RWKV_HEAD:(p + 1) * RWKV_HEAD, :]
                outs[d][0, pl.ds(base, grp), cols] = y_win
        return carry

    lax.fori_loop(0, SCAN_NG, group, 0)


def rwkv_scan(xt, v_src):
    nb, _, nt, _, _ = xt.shape
    s = v_src.shape[1]
    c = RWKV_WIDTH
    tb = SCAN_TB
    nct = CTX_LEN // tb

    def bwd(j):
        return jnp.where(j < nct, nct - 1 - j, nt - 1 - (j - nct))

    gather, expand, last = scan_selectors()
    whole = pl.BlockSpec(memory_space=pltpu.VMEM)
    y_shape = jax.ShapeDtypeStruct((nb, s, c), F32)
    return pl.pallas_call(
        _scan_kernel,
        grid=(nb, nt),
        in_specs=[pl.BlockSpec((1, None, None, SCAN_ROWS, SCAN_K), lambda b, j: (b, 0, j, 0, 0)),
                  pl.BlockSpec((1, None, None, SCAN_ROWS, SCAN_K), lambda b, j: (b, 1, bwd(j), 0, 0)),
                  pl.BlockSpec((1, tb, c), lambda b, j: (b, j, 2)),
                  pl.BlockSpec((1, tb, c), lambda b, j: (b, bwd(j), 2)),
                  whole, whole, whole],
        out_specs=[pl.BlockSpec((1, tb, c), lambda b, j: (b, j, 0)),
                   pl.BlockSpec((1, tb, c), lambda b, j: (b, bwd(j), 0))],
        out_shape=[y_shape, y_shape],
        scratch_shapes=[pltpu.VMEM((2 * SCAN_PAIRS, RWKV_HEAD, LANES), F32),
                        pltpu.VMEM((2, SCAN_NG, SCAN_ROWS, LANES), BF16),
                        pltpu.VMEM((2, SCAN_STEP_ROWS, SCAN_GROUP * LANES), F32),
                        pltpu.VMEM((2, SCAN_PAIRS * RWKV_HEAD, LANES), F32)],
        compiler_params=_cp(("parallel", "arbitrary")),
        name="rwkv_scan",
    )(xt, xt, v_src, v_src, gather, expand, last)


def _readout_kernel(yf_ref, yb_ref, r_ref, k_ref, v_ref, gd_ref,
                    cb_ref, cc_ref, cx_ref, pc_ref, px_ref, nc_ref, nx_ref,
                    gup_ref, rk_ref, lw_ref, lb_ref, cw_ref, o_ref):
    i = pl.program_id(1)
    n_tiles = pl.num_programs(1)
    ones = _head_ones(RWKV_WIDTH, RWKV_HEAD)
    y = yf_ref[0] + yb_ref[0]
    mean = _head_sum(y, ones) * (1.0 / RWKV_HEAD)
    yc = y - mean
    var = _head_sum(yc * yc, ones) * (1.0 / RWKV_HEAD)
    out = yc * lax.rsqrt(var + LNX_EPS) * lw_ref[...] + lb_ref[...]
    bonus = _head_sum(r_ref[0] * k_ref[0] * rk_ref[...], ones) * v_ref[0]
    gate = _dot_hi(jax.nn.sigmoid(gd_ref[0]), gup_ref[...])
    o_ref[0, :, 0:RWKV_WIDTH] = ((out + bonus) * gate).astype(BF16)

    u = cc_ref[0] * cx_ref[0]
    tm = u.shape[0]
    row = lax.broadcasted_iota(jnp.int32, u.shape, 0)
    has_prev = jnp.logical_and(i != 0, i != CTX_LEN // tm)
    has_next = jnp.logical_and(i != CTX_LEN // tm - 1, i != n_tiles - 1)
    u_prev_edge = jnp.where(has_prev, pc_ref[0, 7:8, :] * px_ref[0, 7:8, :], 0.0)
    u_next_edge = jnp.where(has_next, nc_ref[0, 0:1, :] * nx_ref[0, 0:1, :], 0.0)
    u_prev = jnp.where(row == 0, u_prev_edge, pltpu.roll(u, 1, 0))
    u_next = jnp.where(row == tm - 1, u_next_edge, pltpu.roll(u, tm - 1, 0))
    conv = u_prev * cw_ref[0:1, :] + u * cw_ref[1:2, :] + u_next * cw_ref[2:3, :]
    o_ref[0, :, RWKV_WIDTH:] = (cb_ref[0] * conv).astype(BF16)


def rwkv_readout(yf, yb, rkv, gd, conv, g_up, r_k, lnx_w, lnx_b, conv_w):
    nb, s, c = yf.shape
    tm = TOKEN_TILE
    n_tiles = s // tm
    per8 = tm // 8
    last8 = s // 8 - 1
    tok = lambda col, n=c: pl.BlockSpec((1, tm, n), lambda b, i: (b, i, col))
    prev = lambda col: pl.BlockSpec((1, 8, c), lambda b, i: (b, jnp.maximum(i * per8 - 1, 0), col))
    nxt = lambda col: pl.BlockSpec((1, 8, c), lambda b, i: (b, jnp.minimum((i + 1) * per8, last8), col))
    full = lambda shape: pl.BlockSpec(shape, lambda b, i: (0,) * len(shape))
    return pl.pallas_call(
        _readout_kernel,
        grid=(nb, n_tiles),
        in_specs=[tok(0), tok(0), tok(0), tok(1), tok(2), tok(0, 128),
                  tok(0), tok(1), tok(2), prev(1), prev(2), nxt(1), nxt(2),
                  full((GATE_LORA, c)), full((1, c)), full((1, c)), full((1, c)), full((3, c))],
        out_specs=pl.BlockSpec((1, tm, 2 * c), lambda b, i: (b, i, 0)),
        out_shape=jax.ShapeDtypeStruct((nb, s, 2 * c), BF16),
        compiler_params=_cp(("parallel", "parallel")),
        name="rwkv_readout",
    )(yf, yb, rkv, rkv, rkv, gd, conv, conv, conv, conv, conv, conv, conv,
      g_up, r_k.reshape(1, c), lnx_w.reshape(1, c), lnx_b.reshape(1, c), conv_w)


def _softmax_rows(s):
    m = jnp.max(s, axis=-1, keepdims=True)
    e = jnp.exp(s - m)
    return e / jnp.sum(e, axis=-1, keepdims=True)


def _diff_kernel(q_ref, k_ref, v_ref, lam_ref, sub_ref, o_ref, *, lam_init):
    i = pl.program_id(2)
    lam = lam_ref[...]
    lam_full = (jnp.exp(jnp.sum(lam[0:1] * lam[1:2], axis=1, keepdims=True))
                - jnp.exp(jnp.sum(lam[2:3] * lam[3:4], axis=1, keepdims=True)) + lam_init)
    q = q_ref[0] * (DIFF_HEAD ** -0.5)
    lane = lax.broadcasted_iota(jnp.int32, q.shape, 1)
    q1 = jnp.where(lane < DIFF_HEAD, q, jnp.zeros_like(q))
    q2 = jnp.where(lane < DIFF_HEAD, jnp.zeros_like(q), q)

    def attend(n_keys):
        k = k_ref[0, 0:n_keys, :]
        v = v_ref[0, 0:n_keys, :]

        scores = [_dot_nt(qm, k) for qm in (q1, q2)]
        expd = [jnp.exp(s - jnp.max(s, axis=-1, keepdims=True)) for s in scores]
        sm = [_dot(e.astype(BF16), v) / jnp.sum(e, axis=-1, keepdims=True) for e in expd]
        o = sm[0] - lam_full * sm[1]
        ms = jnp.mean(o * o, axis=-1, keepdims=True)
        o = o * lax.rsqrt(ms + SUBLN_EPS) * sub_ref[...] * (1.0 - lam_init)
        o_ref[0] = o.astype(BF16)

    @pl.when(i == 0)
    def _():
        attend(CTX_LEN)

    @pl.when(i != 0)
    def _():
        attend(k_ref.shape[1])


def diff_attention(dq, dk, dv, lam, subln, lam_init):
    nb, s, _ = dq.shape
    tm = TOKEN_TILE
    return pl.pallas_call(
        functools.partial(_diff_kernel, lam_init=lam_init),
        grid=(nb, DIFF_HEADS, s // tm),
        in_specs=[pl.BlockSpec((1, tm, LANES), lambda b, h, i: (b, i, h)),
                  pl.BlockSpec((1, s, LANES), lambda b, h, i: (b, 0, h)),
                  pl.BlockSpec((1, s, LANES), lambda b, h, i: (b, 0, h)),
                  pl.BlockSpec((4, DIFF_HEAD), lambda b, h, i: (0, 0)),
                  pl.BlockSpec((1, DIFF_VHEAD), lambda b, h, i: (0, 0))],
        out_specs=pl.BlockSpec((1, tm, LANES), lambda b, h, i: (b, i, h)),
        out_shape=jax.ShapeDtypeStruct((nb, s, DIFF_WIDTH), BF16),
        compiler_params=_cp(("parallel", "parallel", "parallel")),
        name="diff_attention",
    )(dq, dk, dv, lam, subln.reshape(1, DIFF_VHEAD))


def na_bias_table(rel_bias, rows):
    kr = min(NA_MAX_ROWS, rows)
    cols = jnp.arange(GRID_W)
    col_start = jnp.clip(cols - NA_COLS // 2, 0, GRID_W - NA_COLS)
    kc = jnp.arange(GRID_W)
    inside = (kc[None, :] >= col_start[:, None]) & (kc[None, :] < col_start[:, None] + NA_COLS)
    col_off = kc[None, :] - cols[:, None] + (NA_COLS - 1)
    row_off = jnp.arange(kr)[None, :] - jnp.arange(NA_MAX_ROWS)[:, None] + (NA_MAX_ROWS - 1)
    pick_row = (row_off[:, :, None] == jnp.arange(2 * NA_MAX_ROWS - 1)).astype(F32)
    pick_col = ((col_off[:, :, None] == jnp.arange(2 * NA_COLS - 1)) & inside[:, :, None]).astype(F32)
    b = jnp.einsum('hro,pjr,cko->hpcjk', rel_bias.astype(F32), pick_row, pick_col, precision=HI)
    b = jnp.where(inside[None, None, :, None, :], b, NEG_BIG)
    return b.reshape(rel_bias.shape[0], NA_MAX_ROWS, GRID_W, kr * GRID_W)


def _na_kernel(q_ref, k_ref, v_ref, bias_ref, o_ref, *, rows):
    kr = min(NA_MAX_ROWS, rows)
    scale = NA_HEAD ** -0.5
    w = GRID_W
    lane_q = lax.broadcasted_iota(jnp.int32, (2 * w, LANES), 1)
    row_q = lax.broadcasted_iota(jnp.int32, (2 * w, LANES), 0)
    own = (lane_q < NA_HEAD) == (row_q < w)
    first_head = lax.broadcasted_iota(jnp.int32, (w, LANES), 1) < NA_HEAD
    k_ctx = k_ref[0, 0:CTX_LEN, :]
    v_ctx = v_ref[0, 0:CTX_LEN, :]

    def two_heads(q):
        q2 = jnp.concatenate([q, q], axis=0)
        return jnp.where(own, q2, jnp.zeros_like(q2))

    def merge(o):
        return jnp.where(first_head, o[0:w], o[w:2 * w])

    def body(it, carry):
        todo = []
        for u in range(NA_ROWS_PER_ITER):
            r = it * NA_ROWS_PER_ITER + u
            r_start = jnp.clip(r - kr // 2, 0, rows - kr)
            q = two_heads(q_ref[0, pl.ds(pl.multiple_of(CTX_LEN + r * w, w), w), :])
            base = pl.multiple_of(CTX_LEN + r_start * w, w)
            place = r - r_start
            bias = jnp.concatenate([bias_ref[0, place], bias_ref[1, place]], axis=0)
            s_loc = _dot_nt(q, k_ref[0, pl.ds(base, kr * w), :]) * scale + bias
            s_ctx = _dot_nt(q, k_ctx) * scale
            todo.append((r, base, s_loc, s_ctx))
        probs = []
        for r, base, s_loc, s_ctx in todo:
            m = jnp.maximum(jnp.max(s_loc, axis=-1, keepdims=True), jnp.max(s_ctx, axis=-1, keepdims=True))
            e_loc = jnp.exp(s_loc - m)
            e_ctx = jnp.exp(s_ctx - m)
            z = jnp.sum(e_loc, axis=-1, keepdims=True) + jnp.sum(e_ctx, axis=-1, keepdims=True)
            probs.append((r, base, (e_loc / z).astype(BF16), (e_ctx / z).astype(BF16)))
        for r, base, p_loc, p_ctx in probs:
            o = _dot(p_ctx, v_ctx) + _dot(p_loc, v_ref[0, pl.ds(base, kr * w), :])
            o_ref[0, pl.ds(pl.multiple_of(CTX_LEN + r * w, w), w), :] = merge(o).astype(BF16)
        return carry

    lax.fori_loop(0, rows // NA_ROWS_PER_ITER, body, 0)

    for blk in range(CTX_LEN // w):
        q = two_heads(q_ref[0, blk * w:(blk + 1) * w, :])
        p = _softmax_rows(_dot_nt(q, k_ctx) * scale)
        o_ref[0, blk * w:(blk + 1) * w, :] = merge(_dot(p.astype(BF16), v_ctx)).astype(BF16)


def na_attention(nq, nk, nv, bias_tab, rows):
    nb, s, _ = nq.shape
    kr = min(NA_MAX_ROWS, rows)
    seq = pl.BlockSpec((1, s, LANES), lambda b, p: (b, 0, p))
    return pl.pallas_call(
        functools.partial(_na_kernel, rows=rows),
        grid=(nb, NA_HEADS // 2),
        in_specs=[seq, seq, seq,
                  pl.BlockSpec((2, NA_MAX_ROWS, GRID_W, kr * GRID_W), lambda b, p: (p, 0, 0, 0))],
        out_specs=seq,
        out_shape=jax.ShapeDtypeStruct((nb, s, NA_WIDTH), BF16),
        compiler_params=_cp(("parallel", "parallel")),
        name="na_attention",
    )(nq, nk, nv, bias_tab)


def _route(logits):
    lane = lax.broadcasted_iota(jnp.int32, logits.shape, 1)
    big = jnp.int32(LANES)
    is_g = jnp.logical_and(lane >= N_EXPERTS, lane < N_EXPERTS + N_GROUPS)
    lg = jnp.where(is_g, logits, NEG_BIG)
    mg = jnp.max(lg, axis=-1, keepdims=True)
    zg = jnp.sum(jnp.where(is_g, jnp.exp(lg - mg), 0.0), axis=-1, keepdims=True)
    pg_top = 1.0 / zg
    g_sel = jnp.min(jnp.where(jnp.logical_and(is_g, lg == mg), lane, big), axis=-1, keepdims=True) - N_EXPERTS
    in_g = jnp.logical_and(lane < N_EXPERTS, (lane // EXPERTS_PER_GROUP) == g_sel)
    le = jnp.where(in_g, logits, NEG_BIG)
    m1 = jnp.max(le, axis=-1, keepdims=True)
    i1 = jnp.min(jnp.where(jnp.logical_and(in_g, le == m1), lane, big), axis=-1, keepdims=True)
    rest = jnp.logical_and(in_g, lane != i1)
    le2 = jnp.where(rest, logits, NEG_BIG)
    m2 = jnp.max(le2, axis=-1, keepdims=True)
    i2 = jnp.min(jnp.where(jnp.logical_and(rest, le2 == m2), lane, big), axis=-1, keepdims=True)
    e2 = jnp.exp(m2 - m1)
    w1 = pg_top / (1.0 + e2)
    w2 = pg_top * e2 / (1.0 + e2)
    return (jnp.where(lane == i1, w1, 0.0) + jnp.where(lane == i2, w2, 0.0)
            + jnp.where(lane == N_EXPERTS, g_sel.astype(F32), 0.0))


def _outproj_kernel(x_ref, oa_ref, ob_ref, mod_ref, g_ref, w_ref, wr_ref, br_ref,
                    xo_ref, hx_ref):
    m = mod_ref[0]
    half = oa_ref.shape[2]
    d = x_ref.shape[2]
    tm = x_ref.shape[1]
    parts = [slice(0, tm // 2), slice(tm // 2, tm)]
    ys = [_dot(oa_ref[0, p, :], w_ref[0:half, :]) + _dot(ob_ref[0, p, :], w_ref[half:, :]) for p in parts]
    xs = [x_ref[0, p, :] + m[2:3] * y for p, y in zip(parts, ys)]
    hs = [_norm_mod(x, g_ref[...], m[3:4], m[4:5]) for x in xs]
    logits = [_dot_hi(h, wr_ref[...]) + br_ref[...] for h in hs]
    for p, x, h, lg in zip(parts, xs, hs, logits):
        xo_ref[0, p, :] = x
        hx_ref[0, p, 0:d] = h
        hx_ref[0, p, d:] = _route(lg)


def outproj_route(x, oa, ob, ob_col, modl, g, w_out, w_route, b_route):
    nb, s, d = x.shape
    tm = TOKEN_TILE
    half = d // 2
    tok = lambda n, col=0: pl.BlockSpec((1, tm, n), lambda b, i: (b, i, col))
    full = lambda shape: pl.BlockSpec(shape, lambda b, i: (0,) * len(shape))
    return pl.pallas_call(
        _outproj_kernel,
        grid=(nb, s // tm),
        in_specs=[tok(d), tok(half), tok(half, ob_col),
                  pl.BlockSpec((1, 8, d), _mod_index(nb)),
                  full((1, d)), full((d, d)), full((d, LANES)), full((1, LANES))],
        out_specs=[tok(d), tok(d + LANES)],
        out_shape=[jax.ShapeDtypeStruct((nb, s, d), F32),
                   jax.ShapeDtypeStruct((nb, s, d + LANES), F32)],
        compiler_params=_cp(("parallel", "parallel")),
        name="outproj_route",
    )(x, oa, ob, modl, g, w_out, w_route, b_route)


def moe_group_tables(gid):
    n_tok = gid.shape[0]
    n_rows = n_tok + N_GROUPS * MOE_ROWS
    order = jnp.argsort(gid, stable=True).astype(jnp.int32)
    count = jnp.sum((gid[:, None] == jnp.arange(N_GROUPS)[None, :]).astype(jnp.int32), axis=0)
    padded = (count + MOE_ROWS - 1) // MOE_ROWS * MOE_ROWS
    end = jnp.cumsum(padded)
    first_sorted = jnp.cumsum(count) - count
    row = jnp.arange(n_rows, dtype=jnp.int32)
    group = jnp.minimum(jnp.sum((row[:, None] >= end[None, :]).astype(jnp.int32), axis=1), N_GROUPS - 1)
    local = row - (end - padded)[group]
    real = local < count[group]
    src = jnp.where(real, order[jnp.clip(first_sorted[group] + local, 0, n_tok - 1)], 0)
    real_before = first_sorted[group] + jnp.minimum(local, count[group])
    dst = jnp.where(real, src, n_tok + row - real_before)
    tile_group = group[::MOE_ROWS]
    return tile_group.astype(jnp.int32), src.astype(jnp.int32), dst.astype(jnp.int32)


def _moe_group_kernel(grp_ref, src_ref, dst_ref, hx_hbm, w1_ref, w3_ref, w2_ref, y_hbm,
                      hbuf, ybuf, in_sem, out_sem):
    i = pl.program_id(0)
    n = pl.num_programs(0)
    slot = lax.rem(i, 2)
    d = w1_ref.shape[1]

    def row_in(tile, sl, r):
        return pltpu.make_async_copy(hx_hbm.at[pl.ds(src_ref[tile * MOE_ROWS + r], 1)],
                                     hbuf.at[sl, pl.ds(r, 1)], in_sem.at[sl])

    def row_out(tile, sl, r):
        return pltpu.make_async_copy(ybuf.at[sl, pl.ds(r, 1)],
                                     y_hbm.at[pl.ds(dst_ref[tile * MOE_ROWS + r], 1)], out_sem.at[sl])

    def start_all(make, tile, sl):
        for r in range(MOE_ROWS):
            make(tile, sl, r).start()

    def wait_in(sl):
        pltpu.make_async_copy(hx_hbm.at[pl.ds(0, MOE_ROWS)], hbuf.at[sl], in_sem.at[sl]).wait()

    def wait_out(sl):
        pltpu.make_async_copy(ybuf.at[sl], y_hbm.at[pl.ds(0, MOE_ROWS)], out_sem.at[sl]).wait()

    @pl.when(i == 0)
    def _():
        start_all(row_in, 0, 0)

    start_all(row_in, jnp.minimum(i + 1, n - 1), 1 - slot)
    wait_in(slot)

    @pl.when(i >= 2)
    def _():
        wait_out(slot)

    rows = hbuf[slot]
    h = rows[:, 0:d].astype(BF16)
    comb = rows[:, d:]
    lane = lax.broadcasted_iota(jnp.int32, comb.shape, 1)
    first_expert = grp_ref[i] * EXPERTS_PER_GROUP
    y = jnp.zeros((MOE_ROWS, d), F32)
    for j in range(EXPERTS_PER_GROUP):
        a = _dot(h, w1_ref[j])
        a = a * jax.nn.sigmoid(a) * _dot(h, w3_ref[j])
        gate = jnp.sum(jnp.where(lane == first_expert + j, comb, 0.0), axis=-1, keepdims=True)
        y = y + gate * _dot(a.astype(BF16), w2_ref[j])
    ybuf[slot] = y
    start_all(row_out, i, slot)

    @pl.when(i == n - 1)
    def _():
        wait_in(1 - slot)
        wait_out(1 - slot)
        wait_out(slot)


def moe_grouped(hx, w1, w3, w2):
    n_tok, width = hx.shape
    d = width - LANES
    tile_group, src, dst = moe_group_tables(hx[:, d + N_EXPERTS].astype(jnp.int32))
    n_rows = src.shape[0]
    n_tiles = n_rows // MOE_ROWS
    assert n_tiles >= 2
    grouped = lambda arr: arr.reshape((N_GROUPS, EXPERTS_PER_GROUP) + arr.shape[1:])
    wspec = lambda a, b: pl.BlockSpec((None, EXPERTS_PER_GROUP, a, b), lambda i, grp, s_, d_: (grp[i], 0, 0, 0))
    return pl.pallas_call(
        _moe_group_kernel,
        grid_spec=pltpu.PrefetchScalarGridSpec(
            num_scalar_prefetch=3,
            grid=(n_tiles,),
            in_specs=[pl.BlockSpec(memory_space=pl.ANY),
                      wspec(d, D_EXPERT), wspec(d, D_EXPERT), wspec(D_EXPERT, d)],
            out_specs=pl.BlockSpec(memory_space=pl.ANY),
            scratch_shapes=[pltpu.VMEM((2, MOE_ROWS, width), F32),
                            pltpu.VMEM((2, MOE_ROWS, d), F32),
                            pltpu.SemaphoreType.DMA((2,)),
                            pltpu.SemaphoreType.DMA((2,))]),
        out_shape=jax.ShapeDtypeStruct((n_rows, d), F32),
        compiler_params=_cp(("arbitrary",)),
        name="moe_grouped",
    )(tile_group, src, dst, hx, grouped(w1), grouped(w3), grouped(w2))


def _moe_residual_kernel(x_ref, y_ref, mod_ref, o_ref):
    o_ref[0] = x_ref[0] + mod_ref[0][5:6] * y_ref[...]


def moe_residual(x, y, modl):
    nb, s, d = x.shape
    tm = TOKEN_TILE
    per_row = s // tm
    tok = pl.BlockSpec((1, tm, d), lambda b, i: (b, i, 0))
    return pl.pallas_call(
        _moe_residual_kernel,
        grid=(nb, per_row),
        in_specs=[tok,
                  pl.BlockSpec((tm, d), lambda b, i: (b * per_row + i, 0)),
                  pl.BlockSpec((1, 8, d), _mod_index(nb))],
        out_specs=tok,
        out_shape=jax.ShapeDtypeStruct((nb, s, d), F32),
        compiler_params=_cp(("parallel", "parallel")),
        name="moe_residual",
    )(x, y, modl)


def _final_kernel(x_ref, g_ref, o_ref):
    x = x_ref[0]
    ms = jnp.mean(x * x, axis=-1, keepdims=True)
    o_ref[0] = x * lax.rsqrt(ms + RMS_EPS) * g_ref[...]


def final_norm(x, g):
    nb, s, d = x.shape
    tm = TOKEN_TILE
    skip = CTX_LEN // tm
    return pl.pallas_call(
        _final_kernel,
        grid=(nb, (s - CTX_LEN) // tm),
        in_specs=[pl.BlockSpec((1, tm, d), lambda b, i: (b, i + skip, 0)),
                  pl.BlockSpec((1, d), lambda b, i: (0, 0))],
        out_specs=pl.BlockSpec((1, tm, d), lambda b, i: (b, i, 0)),
        out_shape=jax.ShapeDtypeStruct((nb, s - CTX_LEN, d), F32),
        compiler_params=_cp(("parallel", "parallel")),
        name="final_norm",
    )(x, g.reshape(1, d))


def kernel(x, c, ctx, c_ctx, ada_w, ada_b, norm_g, final_g, ev_w_in, ev_w_out, ev_decay_w0, ev_decay_up, ev_iclr_a0, ev_iclr_up, ev_gate_up, ev_k_k, ev_k_a, ev_r_k, ev_lnx_w, ev_lnx_b, ev_conv_w, od_w_in, od_w_out, od_lambda, od_subln, od_rel_bias, moe_wg, moe_bg, moe_we, moe_be, moe_w1, moe_w3, moe_w2):
    nb, seq, d = x.shape
    rows = seq // GRID_W
    depth = ada_w.shape[0]
    assert nb <= 15 and seq % TOKEN_TILE == 0 and ctx.shape[1] == CTX_LEN

    xs = jnp.concatenate([ctx, x], axis=1)
    cond = jnp.zeros((16, d), F32).at[:nb].set(c).at[nb].set(c_ctx)
    mod = ada_all(cond, ada_w, ada_b)
    mod = mod[:, :nb + 1].reshape(depth, nb + 1, 6, d)
    mod = jnp.concatenate([mod, jnp.zeros((depth, nb + 1, 2, d), F32)], axis=2)
    cos_t, sin_t = rope_tables(seq)

    for l in range(depth):
        i = l // 2
        modl = mod[l]
        g1 = norm_g[l, 0].reshape(1, d)
        g2 = norm_g[l, 1].reshape(1, d)
        if l % 2 == 0:
            rkv, lora, gd, conv = inproj_even(xs, modl, g1, ev_w_in[i].astype(BF16))
            xt = rwkv_prep(rkv, lora, ev_decay_w0[i], ev_decay_up[i], ev_iclr_a0[i],
                           ev_iclr_up[i], ev_k_k[i], ev_k_a[i])
            yf, yb = rwkv_scan(xt, rkv)
            o = rwkv_readout(yf, yb, rkv, gd, conv, ev_gate_up[i], ev_r_k[i], ev_lnx_w[i],
                             ev_lnx_b[i], ev_conv_w[i])
            oa, ob, ob_col = o, o, 1
            w_out = ev_w_out[i]
        else:
            dq, dk, dv, nq, nk, nv = inproj_odd(xs, modl, g1, od_w_in[i].astype(BF16), cos_t, sin_t)
            lam_init = 0.8 - 0.6 * math.exp(-0.3 * l)
            oa = diff_attention(dq, dk, dv, od_lambda[i], od_subln[i], lam_init)
            ob = na_attention(nq, nk, nv, na_bias_table(od_rel_bias[i], rows), rows)
            ob_col = 0
            w_out = od_w_out[i]
        w_route = jnp.concatenate(
            [moe_we[l], moe_wg[l], jnp.zeros((d, LANES - N_EXPERTS - N_GROUPS), F32)], axis=1)
        b_route = jnp.concatenate(
            [moe_be[l], moe_bg[l], jnp.zeros((LANES - N_EXPERTS - N_GROUPS,), F32)]).reshape(1, LANES)
        xs, hx = outproj_route(xs, oa, ob, ob_col, modl, g2, w_out.astype(BF16), w_route, b_route)
        y = moe_grouped(hx.reshape(nb * xs.shape[1], d + LANES), moe_w1[l].astype(BF16),
                        moe_w3[l].astype(BF16), moe_w2[l].astype(BF16))
        xs = moe_residual(xs, y, modl)
    return final_norm(xs, final_g)
```

```python
import functools
import math

import jax
import jax.numpy as jnp
from jax import lax
from jax.experimental import pallas as pl
from jax.experimental.pallas import tpu as pltpu

F32 = jnp.float32
BF16 = jnp.bfloat16
HI = lax.Precision.HIGHEST

D_MODEL = 1024
DEPTH = 4
GRID_W = 64
CTX_LEN = 256

RWKV_WIDTH = 512
RWKV_HEAD = 64
RWKV_HEADS = 8
DECAY_LORA = 64
ICLR_LORA = 64
GATE_LORA = 128
CONV_WIDTH = 512
LNX_EPS = 64e-5

DIFF_HEADS = 4
DIFF_HEAD = 64
DIFF_VHEAD = 128
DIFF_WIDTH = 512
NA_HEADS = 8
NA_HEAD = 64
NA_WIDTH = 512
NA_MAX_ROWS = 8
NA_COLS = 16
ROPE_BASE = 10000.0
SUBLN_EPS = 1e-5

N_GROUPS = 4
EXPERTS_PER_GROUP = 4
N_EXPERTS = 16
D_EXPERT = 512
RMS_EPS = 1e-6

EVEN_PROJ = 3456
ODD_PROJ = 3072

LANES = 128
TOKEN_TILE = 256
MOE_ROWS = 256
NA_ROWS_PER_ITER = 4
SCAN_TB = 128
SCAN_GROUP = 16
SCAN_VECS = 5
SCAN_PAIRS = RWKV_HEADS // 2
SCAN_ROWS = SCAN_VECS * SCAN_PAIRS * RWKV_HEAD
SCAN_STEP_ROWS = (SCAN_VECS - 1) * SCAN_PAIRS * RWKV_HEAD
SCAN_NG = SCAN_TB // SCAN_GROUP
VMEM_LIMIT = 56 * 1024 * 1024
NEG_BIG = -1e30


def _cp(sem, vmem=VMEM_LIMIT):
    return pltpu.CompilerParams(dimension_semantics=sem, vmem_limit_bytes=vmem)


def _dot(a, b):
    return jnp.dot(a, b, preferred_element_type=F32)


def _dot_hi(a, b):
    return jnp.dot(a, b, preferred_element_type=F32, precision=HI)


def _dot_nt(a, b):
    return lax.dot_general(a, b, (((1,), (1,)), ((), ())), preferred_element_type=F32)


def _split3(x):
    hi = x.astype(BF16)
    r1 = x - hi.astype(F32)
    mid = r1.astype(BF16)
    lo = (r1 - mid.astype(F32)).astype(BF16)
    return hi, mid, lo


def _head_ones(width, head):
    i = (lax.broadcasted_iota(jnp.int32, (3 * width, width), 0) % width) // head
    j = lax.broadcasted_iota(jnp.int32, (3 * width, width), 1) // head
    return (i == j).astype(BF16)


def _head_sum(x, ones3):
    return _dot(jnp.concatenate(_split3(x), axis=1), ones3)


def _ada_kernel(cond_ref, w_ref, b_ref, o_ref):
    cond = cond_ref[...]
    s = cond * jax.nn.sigmoid(cond)
    o_ref[0] = _dot_hi(s, w_ref[0]) + b_ref[0]


def ada_all(cond, ada_w, ada_b):
    depth, d, n = ada_w.shape
    tn = 1536
    return pl.pallas_call(
        _ada_kernel,
        grid=(depth, n // tn),
        in_specs=[pl.BlockSpec((16, d), lambda l, j: (0, 0)),
                  pl.BlockSpec((1, d, tn), lambda l, j: (l, 0, j)),
                  pl.BlockSpec((1, 1, tn), lambda l, j: (l, 0, j))],
        out_specs=pl.BlockSpec((1, 16, tn), lambda l, j: (l, 0, j)),
        out_shape=jax.ShapeDtypeStruct((depth, 16, n), F32),
        compiler_params=_cp(("parallel", "parallel")),
        name="ada_mod",
    )(cond, ada_w, ada_b.reshape(depth, 1, n))


def _mod_index(n_batch):
    return lambda b, i: (jnp.where(i == 0, n_batch, b), 0, 0)


def _norm_mod(x, g, shift, scale):
    ms = jnp.mean(x * x, axis=-1, keepdims=True)
    return x * lax.rsqrt(ms + RMS_EPS) * g * (1.0 + scale) + shift


def _inproj_even_kernel(x_ref, mod_ref, g_ref, w_ref, rkv_ref, lora_ref, gd_ref, conv_ref):
    m = mod_ref[0]
    h = _norm_mod(x_ref[0], g_ref[...], m[0:1], m[1:2]).astype(BF16)
    p = _dot(h, w_ref[...])
    rkv_ref[0] = p[:, 0:1536]
    lora_ref[0] = p[:, 1536:1792]
    gd_ref[0] = p[:, 1792:1920]
    conv_ref[0] = p[:, 1920:3456]


def inproj_even(x, modl, g, w):
    nb, s, d = x.shape
    tm = TOKEN_TILE
    tok = lambda n: pl.BlockSpec((1, tm, n), lambda b, i: (b, i, 0))
    return pl.pallas_call(
        _inproj_even_kernel,
        grid=(nb, s // tm),
        in_specs=[tok(d),
                  pl.BlockSpec((1, 8, d), _mod_index(nb)),
                  pl.BlockSpec((1, d), lambda b, i: (0, 0)),
                  pl.BlockSpec((d, EVEN_PROJ), lambda b, i: (0, 0))],
        out_specs=[tok(1536), tok(256), tok(128), tok(1536)],
        out_shape=[jax.ShapeDtypeStruct((nb, s, 1536), F32),
                   jax.ShapeDtypeStruct((nb, s, 256), F32),
                   jax.ShapeDtypeStruct((nb, s, 128), F32),
                   jax.ShapeDtypeStruct((nb, s, 1536), F32)],
        compiler_params=_cp(("parallel", "parallel")),
        name="inproj_even",
    )(x, modl, g, w)


def _rope(x, cos, sin_signed):
    lane = lax.broadcasted_iota(jnp.int32, x.shape, 1)
    partner = jnp.where((lane % 32) < 16, pltpu.roll(x, LANES - 16, 1), pltpu.roll(x, 16, 1))
    return x * cos + partner * sin_signed


def _inproj_odd_kernel(x_ref, mod_ref, g_ref, w_ref, cos_ref, sin_ref,
                       dq_ref, dk_ref, dv_ref, nq_ref, nk_ref, nv_ref):
    m = mod_ref[0]
    h = _norm_mod(x_ref[0], g_ref[...], m[0:1], m[1:2]).astype(BF16)
    p = _dot(h, w_ref[...])
    cos = cos_ref[...]
    sin = sin_ref[...]
    for out_ref, base in ((dq_ref, 0), (dk_ref, 512)):
        for c in range(4):
            blk = p[:, base + c * LANES: base + (c + 1) * LANES]
            out_ref[0, :, c * LANES:(c + 1) * LANES] = _rope(blk, cos, sin).astype(BF16)
    dv_ref[0] = p[:, 1024:1536].astype(BF16)
    nq_ref[0] = p[:, 1536:2048].astype(BF16)
    nk_ref[0] = p[:, 2048:2560].astype(BF16)
    nv_ref[0] = p[:, 2560:3072].astype(BF16)


def inproj_odd(x, modl, g, w, cos_t, sin_t):
    nb, s, d = x.shape
    tm = TOKEN_TILE
    tok = lambda n: pl.BlockSpec((1, tm, n), lambda b, i: (b, i, 0))
    tab = pl.BlockSpec((tm, LANES), lambda b, i: (i, 0))
    return pl.pallas_call(
        _inproj_odd_kernel,
        grid=(nb, s // tm),
        in_specs=[tok(d),
                  pl.BlockSpec((1, 8, d), _mod_index(nb)),
                  pl.BlockSpec((1, d), lambda b, i: (0, 0)),
                  pl.BlockSpec((d, ODD_PROJ), lambda b, i: (0, 0)),
                  tab, tab],
        out_specs=[tok(512)] * 6,
        out_shape=[jax.ShapeDtypeStruct((nb, s, 512), BF16)] * 6,
        compiler_params=_cp(("parallel", "parallel")),
        name="inproj_odd",
    )(x, modl, g, w, cos_t, sin_t)


def rope_tables(seq):
    n_freq = DIFF_HEAD // 4
    inv_freq = ROPE_BASE ** (-jnp.arange(n_freq, dtype=F32) / n_freq)
    t = jnp.arange(seq, dtype=jnp.int32)
    pos = jnp.stack([t // GRID_W, t % GRID_W], axis=-1).astype(F32)
    ang = pos[:, :, None] * inv_freq
    cos, sin = jnp.cos(ang), jnp.sin(ang)
    cos64 = jnp.concatenate([cos[:, 0], cos[:, 0], cos[:, 1], cos[:, 1]], axis=-1)
    sin64 = jnp.concatenate([-sin[:, 0], sin[:, 0], -sin[:, 1], sin[:, 1]], axis=-1)
    cos_t = jnp.concatenate([jnp.ones((CTX_LEN, 64), F32), cos64], axis=0)
    sin_t = jnp.concatenate([jnp.zeros((CTX_LEN, 64), F32), sin64], axis=0)
    return jnp.tile(cos_t, (1, 2)), jnp.tile(sin_t, (1, 2))


def _prep_kernel(r_ref, k_ref, lora_ref, w0_ref, wup_ref, a0_ref, aup_ref, kk_w_ref, ka_ref, xt_ref):
    k = k_ref[0]
    r = r_ref[0]
    lora = lora_ref[0]
    tb = SCAN_TB
    ones = _head_ones(RWKV_WIDTH, RWKV_HEAD)
    kk = k * kk_w_ref[...]
    ss = _head_sum(kk * kk, ones)
    kk = kk * lax.rsqrt(jnp.maximum(ss, 1e-24))
    ti = lax.broadcasted_iota(jnp.int32, (tb, tb), 0)
    tj = lax.broadcasted_iota(jnp.int32, (tb, tb), 1)
    same_group = (ti // SCAN_GROUP) == (tj // SCAN_GROUP)
    upto = [jnp.tile(jnp.logical_and(same_group, cmp).astype(BF16), (1, 3))
            for cmp in (tj <= ti,
                        tj >= ti)]

    def put(d, vec, x):
        width = SCAN_PAIRS * RWKV_HEAD
        for hl in range(2):
            heads = [x[:, (2 * p + hl) * RWKV_HEAD:(2 * p + hl + 1) * RWKV_HEAD] for p in range(SCAN_PAIRS)]
            for term_i, term in enumerate(_split3(jnp.concatenate(heads, axis=1))):
                xt_ref[0, d, term_i, hl, :, vec * width:(vec + 1) * width] = term

    for d in range(2):
        wd = lora[:, d * 64:(d + 1) * 64]
        ad = lora[:, 128 + d * 64:128 + (d + 1) * 64]
        w = w0_ref[d:d + 1, :] + _dot_hi(jnp.tanh(wd), wup_ref[d])
        log_decay = -math.exp(-0.5) * jax.nn.sigmoid(w)
        a = jax.nn.sigmoid(a0_ref[d:d + 1, :] + _dot_hi(ad, aup_ref[d]))
        log_g = _dot(upto[d], jnp.concatenate(_split3(log_decay), axis=0))
        g = jnp.exp(log_g)
        g_inv = jnp.exp(-log_g)
        put(d, 0, kk * jnp.exp(log_g - log_decay))
        put(d, 1, kk * a * g_inv)
        put(d, 2, k * (1.0 + (a - 1.0) * ka_ref[...]) * g_inv)
        put(d, 3, r * g)
        put(d, 4, g)


def rwkv_prep(rkv, lora, w0, wup, a0, aup, k_k, k_a):
    nb, s, _ = rkv.shape
    tb = SCAN_TB
    c = RWKV_WIDTH
    full = lambda shape: pl.BlockSpec(shape, lambda b, i: (0,) * len(shape))
    return pl.pallas_call(
        _prep_kernel,
        grid=(nb, s // tb),
        in_specs=[pl.BlockSpec((1, tb, c), lambda b, i: (b, i, 0)),
                  pl.BlockSpec((1, tb, c), lambda b, i: (b, i, 1)),
                  pl.BlockSpec((1, tb, 256), lambda b, i: (b, i, 0)),
                  full((2, c)), full((2, DECAY_LORA, c)), full((2, c)), full((2, ICLR_LORA, c)),
                  full((1, c)), full((1, c))],
        out_specs=pl.BlockSpec((1, 2, 3, 2, tb, SCAN_ROWS), lambda b, i: (b, 0, 0, 0, i, 0)),
        out_shape=jax.ShapeDtypeStruct((nb, 2, 3, 2, s, SCAN_ROWS), BF16),
        compiler_params=_cp(("parallel", "parallel")),
        name="rwkv_prep",
    )(rkv, rkv, lora, w0, wup, a0, aup, k_k.reshape(1, c), k_a.reshape(1, c))


def scan_selectors():
    grp = SCAN_GROUP
    j = jnp.arange(3 * 2 * grp)
    j_head, j_step = (j // grp) % 2, j % grp
    col = jnp.arange(grp * LANES)
    col_step, col_head = col // LANES, (col % LANES) // RWKV_HEAD
    base = j_head[:, None] == col_head[None, :]
    expand = jnp.stack([base & (j_step[:, None] == col_step[None, :]),
                        base & (j_step[:, None] == grp - 1 - col_step[None, :])])
    lane_head = jnp.arange(LANES) // RWKV_HEAD
    base_l = j_head[:, None] == lane_head[None, :]
    last = jnp.stack([base_l & (j_step[:, None] == grp - 1), base_l & (j_step[:, None] == 0)])
    return expand.astype(BF16), last.astype(BF16)


def _dot_tn(a, b):
    return lax.dot_general(a, b, (((0,), (0,)), ((), ())), preferred_element_type=F32)


def _scan_kernel(xtf_ref, xtb_ref, vf_ref, vb_ref, expand_ref, last_ref, yf_ref, yb_ref,
                 t_scr, big_scr, end_scr):
    tb, grp = SCAN_TB, SCAN_GROUP
    j = pl.program_id(1)

    @pl.when(j == 0)
    def _():
        t_scr[...] = jnp.zeros_like(t_scr)

    xts = (xtf_ref, xtb_ref)
    vs = (vf_ref, vb_ref)
    outs = (yf_ref, yb_ref)
    step_row = lax.broadcasted_iota(jnp.int32, (grp, LANES), 0)

    def group(g, carry):
        off = g * grp
        for d in range(2):
            rows = pl.ds(pl.multiple_of(off if d == 0 else tb - grp - off, grp), grp)
            slab = jnp.concatenate([xts[d][0, term, hl, rows, :] for term in range(3) for hl in range(2)],
                                   axis=0)
            big_scr[d] = _dot_tn(slab[:, 0:SCAN_STEP_ROWS], expand_ref[d])
            end_scr[d] = _dot_tn(slab[:, SCAN_STEP_ROWS:SCAN_ROWS], last_ref[d])
        for d in range(2):
            base = pl.multiple_of(off if d == 0 else tb - grp - off, grp)
            for p in range(SCAN_PAIRS):
                c = d * SCAN_PAIRS + p
                cols = pl.ds(p * LANES, LANES)
                v_win = vs[d][0, pl.ds(base, grp), cols]
                st = t_scr[c]
                y_win = jnp.zeros((grp, LANES), F32)
                for s in range(grp):
                    i = s if d == 0 else grp - 1 - s
                    col = lambda vec: big_scr[d, (vec * SCAN_PAIRS + p) * RWKV_HEAD:
                                              (vec * SCAN_PAIRS + p + 1) * RWKV_HEAD,
                                              s * LANES:(s + 1) * LANES]
                    s_kk = jnp.sum(st * col(0), axis=0, keepdims=True)
                    st = st - col(1) * s_kk + col(2) * v_win[i:i + 1, :]
                    y = jnp.sum(st * col(3), axis=0, keepdims=True)
                    y_win = jnp.where(step_row == i, y, y_win)
                t_scr[c] = st * end_scr[d, p * RWKV_HEAD:(p + 1) * RWKV_HEAD, :]
                outs[d][0, pl.ds(base, grp), cols] = y_win
        return carry

    lax.fori_loop(0, SCAN_NG, group, 0)


def rwkv_scan(xt, v_src):
    nb = xt.shape[0]
    s = v_src.shape[1]
    c = RWKV_WIDTH
    tb = SCAN_TB
    nt = s // tb
    nct = CTX_LEN // tb

    def bwd(j):
        return jnp.where(j < nct, nct - 1 - j, nt - 1 - (j - nct))

    expand, last = scan_selectors()
    whole = pl.BlockSpec(memory_space=pltpu.VMEM)
    y_shape = jax.ShapeDtypeStruct((nb, s, c), F32)
    return pl.pallas_call(
        _scan_kernel,
        grid=(nb, nt),
        in_specs=[pl.BlockSpec((1, None, 3, 2, tb, SCAN_ROWS), lambda b, j: (b, 0, 0, 0, j, 0)),
                  pl.BlockSpec((1, None, 3, 2, tb, SCAN_ROWS), lambda b, j: (b, 1, 0, 0, bwd(j), 0)),
                  pl.BlockSpec((1, tb, c), lambda b, j: (b, j, 2)),
                  pl.BlockSpec((1, tb, c), lambda b, j: (b, bwd(j), 2)),
                  whole, whole],
        out_specs=[pl.BlockSpec((1, tb, c), lambda b, j: (b, j, 0)),
                   pl.BlockSpec((1, tb, c), lambda b, j: (b, bwd(j), 0))],
        out_shape=[y_shape, y_shape],
        scratch_shapes=[pltpu.VMEM((2 * SCAN_PAIRS, RWKV_HEAD, LANES), F32),
                        pltpu.VMEM((2, SCAN_STEP_ROWS, SCAN_GROUP * LANES), F32),
                        pltpu.VMEM((2, SCAN_PAIRS * RWKV_HEAD, LANES), F32)],
        compiler_params=_cp(("parallel", "arbitrary")),
        name="rwkv_scan",
    )(xt, xt, v_src, v_src, expand, last)


def _readout_kernel(yf_ref, yb_ref, r_ref, k_ref, v_ref, gd_ref,
                    cb_ref, cc_ref, cx_ref, pc_ref, px_ref, nc_ref, nx_ref,
                    gup_ref, rk_ref, lw_ref, lb_ref, cw_ref, o_ref):
    i = pl.program_id(1)
    n_tiles = pl.num_programs(1)
    ones = _head_ones(RWKV_WIDTH, RWKV_HEAD)
    y = yf_ref[0] + yb_ref[0]
    mean = _head_sum(y, ones) * (1.0 / RWKV_HEAD)
    yc = y - mean
    var = _head_sum(yc * yc, ones) * (1.0 / RWKV_HEAD)
    out = yc * lax.rsqrt(var + LNX_EPS) * lw_ref[...] + lb_ref[...]
    bonus = _head_sum(r_ref[0] * k_ref[0] * rk_ref[...], ones) * v_ref[0]
    gate = _dot_hi(jax.nn.sigmoid(gd_ref[0]), gup_ref[...])
    o_ref[0, :, 0:RWKV_WIDTH] = ((out + bonus) * gate).astype(BF16)

    u = cc_ref[0] * cx_ref[0]
    tm = u.shape[0]
    row = lax.broadcasted_iota(jnp.int32, u.shape, 0)
    has_prev = jnp.logical_and(i != 0, i != CTX_LEN // tm)
    has_next = jnp.logical_and(i != CTX_LEN // tm - 1, i != n_tiles - 1)
    u_prev_edge = jnp.where(has_prev, pc_ref[0, 7:8, :] * px_ref[0, 7:8, :], 0.0)
    u_next_edge = jnp.where(has_next, nc_ref[0, 0:1, :] * nx_ref[0, 0:1, :], 0.0)
    u_prev = jnp.where(row == 0, u_prev_edge, pltpu.roll(u, 1, 0))
    u_next = jnp.where(row == tm - 1, u_next_edge, pltpu.roll(u, tm - 1, 0))
    conv = u_prev * cw_ref[0:1, :] + u * cw_ref[1:2, :] + u_next * cw_ref[2:3, :]
    o_ref[0, :, RWKV_WIDTH:] = (cb_ref[0] * conv).astype(BF16)


def rwkv_readout(yf, yb, rkv, gd, conv, g_up, r_k, lnx_w, lnx_b, conv_w):
    nb, s, c = yf.shape
    tm = TOKEN_TILE
    n_tiles = s // tm
    per8 = tm // 8
    last8 = s // 8 - 1
    tok = lambda col, n=c: pl.BlockSpec((1, tm, n), lambda b, i: (b, i, col))
    prev = lambda col: pl.BlockSpec((1, 8, c), lambda b, i: (b, jnp.maximum(i * per8 - 1, 0), col))
    nxt = lambda col: pl.BlockSpec((1, 8, c), lambda b, i: (b, jnp.minimum((i + 1) * per8, last8), col))
    full = lambda shape: pl.BlockSpec(shape, lambda b, i: (0,) * len(shape))
    return pl.pallas_call(
        _readout_kernel,
        grid=(nb, n_tiles),
        in_specs=[tok(0), tok(0), tok(0), tok(1), tok(2), tok(0, 128),
                  tok(0), tok(1), tok(2), prev(1), prev(2), nxt(1), nxt(2),
                  full((GATE_LORA, c)), full((1, c)), full((1, c)), full((1, c)), full((3, c))],
        out_specs=pl.BlockSpec((1, tm, 2 * c), lambda b, i: (b, i, 0)),
        out_shape=jax.ShapeDtypeStruct((nb, s, 2 * c), BF16),
        compiler_params=_cp(("parallel", "parallel")),
        name="rwkv_readout",
    )(yf, yb, rkv, rkv, rkv, gd, conv, conv, conv, conv, conv, conv, conv,
      g_up, r_k.reshape(1, c), lnx_w.reshape(1, c), lnx_b.reshape(1, c), conv_w)


def _softmax_rows(s):
    m = jnp.max(s, axis=-1, keepdims=True)
    e = jnp.exp(s - m)
    return e / jnp.sum(e, axis=-1, keepdims=True)


def _diff_kernel(q_ref, k_ref, v_ref, lam_ref, sub_ref, o_ref, *, lam_init):
    i = pl.program_id(2)
    lam = lam_ref[...]
    lam_full = (jnp.exp(jnp.sum(lam[0:1] * lam[1:2], axis=1, keepdims=True))
                - jnp.exp(jnp.sum(lam[2:3] * lam[3:4], axis=1, keepdims=True)) + lam_init)
    q = q_ref[0] * (DIFF_HEAD ** -0.5)
    lane = lax.broadcasted_iota(jnp.int32, q.shape, 1)
    q1 = jnp.where(lane < DIFF_HEAD, q, jnp.zeros_like(q))
    q2 = jnp.where(lane < DIFF_HEAD, jnp.zeros_like(q), q)

    def attend(n_keys):
        k = k_ref[0, 0:n_keys, :]
        v = v_ref[0, 0:n_keys, :]

        scores = [_dot_nt(qm, k) for qm in (q1, q2)]
        expd = [jnp.exp(s - jnp.max(s, axis=-1, keepdims=True)) for s in scores]
        sm = [_dot(e.astype(BF16), v) / jnp.sum(e, axis=-1, keepdims=True) for e in expd]
        o = sm[0] - lam_full * sm[1]
        ms = jnp.mean(o * o, axis=-1, keepdims=True)
        o = o * lax.rsqrt(ms + SUBLN_EPS) * sub_ref[...] * (1.0 - lam_init)
        o_ref[0] = o.astype(BF16)

    @pl.when(i == 0)
    def _():
        attend(CTX_LEN)

    @pl.when(i != 0)
    def _():
        attend(k_ref.shape[1])


def diff_attention(dq, dk, dv, lam, subln, lam_init):
    nb, s, _ = dq.shape
    tm = TOKEN_TILE
    return pl.pallas_call(
        functools.partial(_diff_kernel, lam_init=lam_init),
        grid=(nb, DIFF_HEADS, s // tm),
        in_specs=[pl.BlockSpec((1, tm, LANES), lambda b, h, i: (b, i, h)),
                  pl.BlockSpec((1, s, LANES), lambda b, h, i: (b, 0, h)),
                  pl.BlockSpec((1, s, LANES), lambda b, h, i: (b, 0, h)),
                  pl.BlockSpec((4, DIFF_HEAD), lambda b, h, i: (0, 0)),
                  pl.BlockSpec((1, DIFF_VHEAD), lambda b, h, i: (0, 0))],
        out_specs=pl.BlockSpec((1, tm, LANES), lambda b, h, i: (b, i, h)),
        out_shape=jax.ShapeDtypeStruct((nb, s, DIFF_WIDTH), BF16),
        compiler_params=_cp(("parallel", "parallel", "parallel")),
        name="diff_attention",
    )(dq, dk, dv, lam, subln.reshape(1, DIFF_VHEAD))


def na_bias_table(rel_bias, rows):
    kr = min(NA_MAX_ROWS, rows)
    cols = jnp.arange(GRID_W)
    col_start = jnp.clip(cols - NA_COLS // 2, 0, GRID_W - NA_COLS)
    kc = jnp.arange(GRID_W)
    inside = (kc[None, :] >= col_start[:, None]) & (kc[None, :] < col_start[:, None] + NA_COLS)
    col_off = kc[None, :] - cols[:, None] + (NA_COLS - 1)
    row_off = jnp.arange(kr)[None, :] - jnp.arange(NA_MAX_ROWS)[:, None] + (NA_MAX_ROWS - 1)
    pick_row = (row_off[:, :, None] == jnp.arange(2 * NA_MAX_ROWS - 1)).astype(F32)
    pick_col = ((col_off[:, :, None] == jnp.arange(2 * NA_COLS - 1)) & inside[:, :, None]).astype(F32)
    b = jnp.einsum('hro,pjr,cko->hpcjk', rel_bias.astype(F32), pick_row, pick_col, precision=HI)
    b = jnp.where(inside[None, None, :, None, :], b, NEG_BIG)
    return b.reshape(rel_bias.shape[0], NA_MAX_ROWS, GRID_W, kr * GRID_W)


def _na_kernel(q_ref, k_ref, v_ref, bias_ref, o_ref, *, rows):
    kr = min(NA_MAX_ROWS, rows)
    scale = NA_HEAD ** -0.5
    w = GRID_W
    lane_q = lax.broadcasted_iota(jnp.int32, (2 * w, LANES), 1)
    row_q = lax.broadcasted_iota(jnp.int32, (2 * w, LANES), 0)
    own = (lane_q < NA_HEAD) == (row_q < w)
    first_head = lax.broadcasted_iota(jnp.int32, (w, LANES), 1) < NA_HEAD
    k_ctx = k_ref[0, 0:CTX_LEN, :]
    v_ctx = v_ref[0, 0:CTX_LEN, :]

    def two_heads(q):
        q2 = jnp.concatenate([q, q], axis=0)
        return jnp.where(own, q2, jnp.zeros_like(q2))

    def merge(o):
        return jnp.where(first_head, o[0:w], o[w:2 * w])

    def body(it, carry):
        todo = []
        for u in range(NA_ROWS_PER_ITER):
            r = it * NA_ROWS_PER_ITER + u
            r_start = jnp.clip(r - kr // 2, 0, rows - kr)
            q = two_heads(q_ref[0, pl.ds(pl.multiple_of(CTX_LEN + r * w, w), w), :])
            base = pl.multiple_of(CTX_LEN + r_start * w, w)
            place = r - r_start
            bias = jnp.concatenate([bias_ref[0, place], bias_ref[1, place]], axis=0)
            s_loc = _dot_nt(q, k_ref[0, pl.ds(base, kr * w), :]) * scale + bias
            s_ctx = _dot_nt(q, k_ctx) * scale
            todo.append((r, base, s_loc, s_ctx))
        probs = []
        for r, base, s_loc, s_ctx in todo:
            m = jnp.maximum(jnp.max(s_loc, axis=-1, keepdims=True), jnp.max(s_ctx, axis=-1, keepdims=True))
            e_loc = jnp.exp(s_loc - m)
            e_ctx = jnp.exp(s_ctx - m)
            z = jnp.sum(e_loc, axis=-1, keepdims=True) + jnp.sum(e_ctx, axis=-1, keepdims=True)
            probs.append((r, base, (e_loc / z).astype(BF16), (e_ctx / z).astype(BF16)))
        for r, base, p_loc, p_ctx in probs:
            o = _dot(p_ctx, v_ctx) + _dot(p_loc, v_ref[0, pl.ds(base, kr * w), :])
            o_ref[0, pl.ds(pl.multiple_of(CTX_LEN + r * w, w), w), :] = merge(o).astype(BF16)
        return carry

    lax.fori_loop(0, rows // NA_ROWS_PER_ITER, body, 0)

    for blk in range(CTX_LEN // w):
        q = two_heads(q_ref[0, blk * w:(blk + 1) * w, :])
        p = _softmax_rows(_dot_nt(q, k_ctx) * scale)
        o_ref[0, blk * w:(blk + 1) * w, :] = merge(_dot(p.astype(BF16), v_ctx)).astype(BF16)


def na_attention(nq, nk, nv, bias_tab, rows):
    nb, s, _ = nq.shape
    kr = min(NA_MAX_ROWS, rows)
    seq = pl.BlockSpec((1, s, LANES), lambda b, p: (b, 0, p))
    return pl.pallas_call(
        functools.partial(_na_kernel, rows=rows),
        grid=(nb, NA_HEADS // 2),
        in_specs=[seq, seq, seq,
                  pl.BlockSpec((2, NA_MAX_ROWS, GRID_W, kr * GRID_W), lambda b, p: (p, 0, 0, 0))],
        out_specs=seq,
        out_shape=jax.ShapeDtypeStruct((nb, s, NA_WIDTH), BF16),
        compiler_params=_cp(("parallel", "parallel")),
        name="na_attention",
    )(nq, nk, nv, bias_tab)


def _route(logits):
    lane = lax.broadcasted_iota(jnp.int32, logits.shape, 1)
    big = jnp.int32(LANES)
    is_g = jnp.logical_and(lane >= N_EXPERTS, lane < N_EXPERTS + N_GROUPS)
    lg = jnp.where(is_g, logits, NEG_BIG)
    mg = jnp.max(lg, axis=-1, keepdims=True)
    zg = jnp.sum(jnp.where(is_g, jnp.exp(lg - mg), 0.0), axis=-1, keepdims=True)
    pg_top = 1.0 / zg
    g_sel = jnp.min(jnp.where(jnp.logical_and(is_g, lg == mg), lane, big), axis=-1, keepdims=True) - N_EXPERTS
    in_g = jnp.logical_and(lane < N_EXPERTS, (lane // EXPERTS_PER_GROUP) == g_sel)
    le = jnp.where(in_g, logits, NEG_BIG)
    m1 = jnp.max(le, axis=-1, keepdims=True)
    i1 = jnp.min(jnp.where(jnp.logical_and(in_g, le == m1), lane, big), axis=-1, keepdims=True)
    rest = jnp.logical_and(in_g, lane != i1)
    le2 = jnp.where(rest, logits, NEG_BIG)
    m2 = jnp.max(le2, axis=-1, keepdims=True)
    i2 = jnp.min(jnp.where(jnp.logical_and(rest, le2 == m2), lane, big), axis=-1, keepdims=True)
    e2 = jnp.exp(m2 - m1)
    w1 = pg_top / (1.0 + e2)
    w2 = pg_top * e2 / (1.0 + e2)
    return (jnp.where(lane == i1, w1, 0.0) + jnp.where(lane == i2, w2, 0.0)
            + jnp.where(lane == N_EXPERTS, g_sel.astype(F32), 0.0))


def _outproj_kernel(x_ref, oa_ref, ob_ref, mod_ref, g_ref, w_ref, wr_ref, br_ref,
                    xo_ref, hx_ref):
    m = mod_ref[0]
    half = oa_ref.shape[2]
    d = x_ref.shape[2]
    tm = x_ref.shape[1]
    parts = [slice(0, tm // 2), slice(tm // 2, tm)]
    ys = [_dot(oa_ref[0, p, :], w_ref[0:half, :]) + _dot(ob_ref[0, p, :], w_ref[half:, :]) for p in parts]
    xs = [x_ref[0, p, :] + m[2:3] * y for p, y in zip(parts, ys)]
    hs = [_norm_mod(x, g_ref[...], m[3:4], m[4:5]) for x in xs]
    logits = [_dot_hi(h, wr_ref[...]) + br_ref[...] for h in hs]
    for p, x, h, lg in zip(parts, xs, hs, logits):
        xo_ref[0, p, :] = x
        hx_ref[0, p, 0:d] = h
        hx_ref[0, p, d:] = _route(lg)


def outproj_route(x, oa, ob, ob_col, modl, g, w_out, w_route, b_route):
    nb, s, d = x.shape
    tm = TOKEN_TILE
    half = d // 2
    tok = lambda n, col=0: pl.BlockSpec((1, tm, n), lambda b, i: (b, i, col))
    full = lambda shape: pl.BlockSpec(shape, lambda b, i: (0,) * len(shape))
    return pl.pallas_call(
        _outproj_kernel,
        grid=(nb, s // tm),
        in_specs=[tok(d), tok(half), tok(half, ob_col),
                  pl.BlockSpec((1, 8, d), _mod_index(nb)),
                  full((1, d)), full((d, d)), full((d, LANES)), full((1, LANES))],
        out_specs=[tok(d), tok(d + LANES)],
        out_shape=[jax.ShapeDtypeStruct((nb, s, d), F32),
                   jax.ShapeDtypeStruct((nb, s, d + LANES), F32)],
        compiler_params=_cp(("parallel", "parallel")),
        name="outproj_route",
    )(x, oa, ob, modl, g, w_out, w_route, b_route)


def moe_group_tables(gid):
    n_tok = gid.shape[0]
    n_rows = n_tok + N_GROUPS * MOE_ROWS
    order = jnp.argsort(gid, stable=True).astype(jnp.int32)
    count = jnp.sum((gid[:, None] == jnp.arange(N_GROUPS)[None, :]).astype(jnp.int32), axis=0)
    padded = (count + MOE_ROWS - 1) // MOE_ROWS * MOE_ROWS
    end = jnp.cumsum(padded)
    first_sorted = jnp.cumsum(count) - count
    row = jnp.arange(n_rows, dtype=jnp.int32)
    group = jnp.minimum(jnp.sum((row[:, None] >= end[None, :]).astype(jnp.int32), axis=1), N_GROUPS - 1)
    local = row - (end - padded)[group]
    real = local < count[group]
    src = jnp.where(real, order[jnp.clip(first_sorted[group] + local, 0, n_tok - 1)], 0)
    real_before = first_sorted[group] + jnp.minimum(local, count[group])
    dst = jnp.where(real, src, n_tok + row - real_before)
    tile_group = group[::MOE_ROWS]
    return tile_group.astype(jnp.int32), src.astype(jnp.int32), dst.astype(jnp.int32)


def _moe_group_kernel(grp_ref, src_ref, dst_ref, hx_hbm, w1_ref, w3_ref, w2_ref, y_hbm,
                      hbuf, ybuf, in_sem, out_sem):
    i = pl.program_id(0)
    n = pl.num_programs(0)
    slot = lax.rem(i, 2)
    d = w1_ref.shape[1]

    def row_in(tile, sl, r):
        return pltpu.make_async_copy(hx_hbm.at[pl.ds(src_ref[tile * MOE_ROWS + r], 1)],
                                     hbuf.at[sl, pl.ds(r, 1)], in_sem.at[sl])

    def row_out(tile, sl, r):
        return pltpu.make_async_copy(ybuf.at[sl, pl.ds(r, 1)],
                                     y_hbm.at[pl.ds(dst_ref[tile * MOE_ROWS + r], 1)], out_sem.at[sl])

    def start_all(make, tile, sl):
        for r in range(MOE_ROWS):
            make(tile, sl, r).start()

    def wait_in(sl):
        pltpu.make_async_copy(hx_hbm.at[pl.ds(0, MOE_ROWS)], hbuf.at[sl], in_sem.at[sl]).wait()

    def wait_out(sl):
        pltpu.make_async_copy(ybuf.at[sl], y_hbm.at[pl.ds(0, MOE_ROWS)], out_sem.at[sl]).wait()

    @pl.when(i == 0)
    def _():
        start_all(row_in, 0, 0)

    start_all(row_in, jnp.minimum(i + 1, n - 1), 1 - slot)
    wait_in(slot)

    @pl.when(i >= 2)
    def _():
        wait_out(slot)

    rows = hbuf[slot]
    h = rows[:, 0:d].astype(BF16)
    comb = rows[:, d:]
    lane = lax.broadcasted_iota(jnp.int32, comb.shape, 1)
    first_expert = grp_ref[i] * EXPERTS_PER_GROUP
    y = jnp.zeros((MOE_ROWS, d), F32)
    for j in range(EXPERTS_PER_GROUP):
        a = _dot(h, w1_ref[j])
        a = a * jax.nn.sigmoid(a) * _dot(h, w3_ref[j])
        gate = jnp.sum(jnp.where(lane == first_expert + j, comb, 0.0), axis=-1, keepdims=True)
        y = y + gate * _dot(a.astype(BF16), w2_ref[j])
    ybuf[slot] = y
    start_all(row_out, i, slot)

    @pl.when(i == n - 1)
    def _():
        wait_in(1 - slot)
        wait_out(1 - slot)
        wait_out(slot)


def moe_grouped(hx, w1, w3, w2):
    n_tok, width = hx.shape
    d = width - LANES
    tile_group, src, dst = moe_group_tables(hx[:, d + N_EXPERTS].astype(jnp.int32))
    n_rows = src.shape[0]
    n_tiles = n_rows // MOE_ROWS
    assert n_tiles >= 2
    grouped = lambda arr: arr.reshape((N_GROUPS, EXPERTS_PER_GROUP) + arr.shape[1:])
    wspec = lambda a, b: pl.BlockSpec((None, EXPERTS_PER_GROUP, a, b), lambda i, grp, s_, d_: (grp[i], 0, 0, 0))
    return pl.pallas_call(
        _moe_group_kernel,
        grid_spec=pltpu.PrefetchScalarGridSpec(
            num_scalar_prefetch=3,
            grid=(n_tiles,),
            in_specs=[pl.BlockSpec(memory_space=pl.ANY),
                      wspec(d, D_EXPERT), wspec(d, D_EXPERT), wspec(D_EXPERT, d)],
            out_specs=pl.BlockSpec(memory_space=pl.ANY),
            scratch_shapes=[pltpu.VMEM((2, MOE_ROWS, width), F32),
                            pltpu.VMEM((2, MOE_ROWS, d), F32),
                            pltpu.SemaphoreType.DMA((2,)),
                            pltpu.SemaphoreType.DMA((2,))]),
        out_shape=jax.ShapeDtypeStruct((n_rows, d), F32),
        compiler_params=_cp(("arbitrary",)),
        name="moe_grouped",
    )(tile_group, src, dst, hx, grouped(w1), grouped(w3), grouped(w2))


def _moe_residual_kernel(x_ref, y_ref, mod_ref, o_ref):
    o_ref[0] = x_ref[0] + mod_ref[0][5:6] * y_ref[...]


def moe_residual(x, y, modl):
    nb, s, d = x.shape
    tm = TOKEN_TILE
    per_row = s // tm
    tok = pl.BlockSpec((1, tm, d), lambda b, i: (b, i, 0))
    return pl.pallas_call(
        _moe_residual_kernel,
        grid=(nb, per_row),
        in_specs=[tok,
                  pl.BlockSpec((tm, d), lambda b, i: (b * per_row + i, 0)),
                  pl.BlockSpec((1, 8, d), _mod_index(nb))],
        out_specs=tok,
        out_shape=jax.ShapeDtypeStruct((nb, s, d), F32),
        compiler_params=_cp(("parallel", "parallel")),
        name="moe_residual",
    )(x, y, modl)


def _final_kernel(x_ref, g_ref, o_ref):
    x = x_ref[0]
    ms = jnp.mean(x * x, axis=-1, keepdims=True)
    o_ref[0] = x * lax.rsqrt(ms + RMS_EPS) * g_ref[...]


def final_norm(x, g):
    nb, s, d = x.shape
    tm = TOKEN_TILE
    skip = CTX_LEN // tm
    return pl.pallas_call(
        _final_kernel,
        grid=(nb, (s - CTX_LEN) // tm),
        in_specs=[pl.BlockSpec((1, tm, d), lambda b, i: (b, i + skip, 0)),
                  pl.BlockSpec((1, d), lambda b, i: (0, 0))],
        out_specs=pl.BlockSpec((1, tm, d), lambda b, i: (b, i, 0)),
        out_shape=jax.ShapeDtypeStruct((nb, s - CTX_LEN, d), F32),
        compiler_params=_cp(("parallel", "parallel")),
        name="final_norm",
    )(x, g.reshape(1, d))


def kernel(x, c, ctx, c_ctx, ada_w, ada_b, norm_g, final_g, ev_w_in, ev_w_out, ev_decay_w0, ev_decay_up, ev_iclr_a0, ev_iclr_up, ev_gate_up, ev_k_k, ev_k_a, ev_r_k, ev_lnx_w, ev_lnx_b, ev_conv_w, od_w_in, od_w_out, od_lambda, od_subln, od_rel_bias, moe_wg, moe_bg, moe_we, moe_be, moe_w1, moe_w3, moe_w2):
    nb, seq, d = x.shape
    rows = seq // GRID_W
    depth = ada_w.shape[0]
    assert nb <= 15 and seq % TOKEN_TILE == 0 and ctx.shape[1] == CTX_LEN

    xs = jnp.concatenate([ctx, x], axis=1)
    cond = jnp.zeros((16, d), F32).at[:nb].set(c).at[nb].set(c_ctx)
    mod = ada_all(cond, ada_w, ada_b)
    mod = mod[:, :nb + 1].reshape(depth, nb + 1, 6, d)
    mod = jnp.concatenate([mod, jnp.zeros((depth, nb + 1, 2, d), F32)], axis=2)
    cos_t, sin_t = rope_tables(seq)

    for l in range(depth):
        i = l // 2
        modl = mod[l]
        g1 = norm_g[l, 0].reshape(1, d)
        g2 = norm_g[l, 1].reshape(1, d)
        if l % 2 == 0:
            rkv, lora, gd, conv = inproj_even(xs, modl, g1, ev_w_in[i].astype(BF16))
            xt = rwkv_prep(rkv, lora, ev_decay_w0[i], ev_decay_up[i], ev_iclr_a0[i],
                           ev_iclr_up[i], ev_k_k[i], ev_k_a[i])
            yf, yb = rwkv_scan(xt, rkv)
            o = rwkv_readout(yf, yb, rkv, gd, conv, ev_gate_up[i], ev_r_k[i], ev_lnx_w[i],
                             ev_lnx_b[i], ev_conv_w[i])
            oa, ob, ob_col = o, o, 1
            w_out = ev_w_out[i]
        else:
            dq, dk, dv, nq, nk, nv = inproj_odd(xs, modl, g1, od_w_in[i].astype(BF16), cos_t, sin_t)
            lam_init = 0.8 - 0.6 * math.exp(-0.3 * l)
            oa = diff_attention(dq, dk, dv, od_lambda[i], od_subln[i], lam_init)
            ob = na_attention(nq, nk, nv, na_bias_table(od_rel_bias[i], rows), rows)
            ob_col = 0
            w_out = od_w_out[i]
        w_route = jnp.concatenate(
            [moe_we[l], moe_wg[l], jnp.zeros((d, LANES - N_EXPERTS - N_GROUPS), F32)], axis=1)
        b_route = jnp.concatenate(
            [moe_be[l], moe_bg[l], jnp.zeros((LANES - N_EXPERTS - N_GROUPS,), F32)]).reshape(1, LANES)
        xs, hx = outproj_route(xs, oa, ob, ob_col, modl, g2, w_out.astype(BF16), w_route, b_route)
        y = moe_grouped(hx.reshape(nb * xs.shape[1], d + LANES), moe_w1[l].astype(BF16),
                        moe_w3[l].astype(BF16), moe_w2[l].astype(BF16))
        xs = moe_residual(xs, y, modl)
    return final_norm(xs, final_g)
```

```python
import functools
import math

import jax
import jax.numpy as jnp
from jax import lax
from jax.experimental import pallas as pl
from jax.experimental.pallas import tpu as pltpu

F32 = jnp.float32
BF16 = jnp.bfloat16
HI = lax.Precision.HIGHEST

D_MODEL = 1024
DEPTH = 4
GRID_W = 64
CTX_LEN = 256

RWKV_WIDTH = 512
RWKV_HEAD = 64
RWKV_HEADS = 8
DECAY_LORA = 64
ICLR_LORA = 64
GATE_LORA = 128
CONV_WIDTH = 512
LNX_EPS = 64e-5

DIFF_HEADS = 4
DIFF_HEAD = 64
DIFF_VHEAD = 128
DIFF_WIDTH = 512
NA_HEADS = 8
NA_HEAD = 64
NA_WIDTH = 512
NA_MAX_ROWS = 8
NA_COLS = 16
ROPE_BASE = 10000.0
SUBLN_EPS = 1e-5

N_GROUPS = 4
EXPERTS_PER_GROUP = 4
N_EXPERTS = 16
D_EXPERT = 512
RMS_EPS = 1e-6

EVEN_PROJ = 3456
ODD_PROJ = 3072

LANES = 128
TOKEN_TILE = 256
MOE_ROWS = 256
NA_ROWS_PER_ITER = 4
SCAN_TB = 128
SCAN_GROUP = 32
SCAN_VECS = 5
SCAN_PAIRS = RWKV_HEADS // 2
SCAN_ROWS = SCAN_VECS * SCAN_PAIRS * RWKV_HEAD
SCAN_STEP_ROWS = (SCAN_VECS - 1) * SCAN_PAIRS * RWKV_HEAD
SCAN_NG = SCAN_TB // SCAN_GROUP
VMEM_LIMIT = 56 * 1024 * 1024
NEG_BIG = -1e30


def _cp(sem, vmem=VMEM_LIMIT):
    return pltpu.CompilerParams(dimension_semantics=sem, vmem_limit_bytes=vmem)


def _dot(a, b):
    return jnp.dot(a, b, preferred_element_type=F32)


def _dot_hi(a, b):
    return jnp.dot(a, b, preferred_element_type=F32, precision=HI)


def _dot_nt(a, b):
    return lax.dot_general(a, b, (((1,), (1,)), ((), ())), preferred_element_type=F32)


def _split3(x):
    hi = x.astype(BF16)
    r1 = x - hi.astype(F32)
    mid = r1.astype(BF16)
    lo = (r1 - mid.astype(F32)).astype(BF16)
    return hi, mid, lo


def _head_ones(width, head):
    i = (lax.broadcasted_iota(jnp.int32, (3 * width, width), 0) % width) // head
    j = lax.broadcasted_iota(jnp.int32, (3 * width, width), 1) // head
    return (i == j).astype(BF16)


def _head_sum(x, ones3):
    return _dot(jnp.concatenate(_split3(x), axis=1), ones3)


def _ada_kernel(cond_ref, w_ref, b_ref, o_ref):
    cond = cond_ref[...]
    s = cond * jax.nn.sigmoid(cond)
    o_ref[0] = _dot_hi(s, w_ref[0]) + b_ref[0]


def ada_all(cond, ada_w, ada_b):
    depth, d, n = ada_w.shape
    tn = 1536
    return pl.pallas_call(
        _ada_kernel,
        grid=(depth, n // tn),
        in_specs=[pl.BlockSpec((16, d), lambda l, j: (0, 0)),
                  pl.BlockSpec((1, d, tn), lambda l, j: (l, 0, j)),
                  pl.BlockSpec((1, 1, tn), lambda l, j: (l, 0, j))],
        out_specs=pl.BlockSpec((1, 16, tn), lambda l, j: (l, 0, j)),
        out_shape=jax.ShapeDtypeStruct((depth, 16, n), F32),
        compiler_params=_cp(("parallel", "parallel")),
        name="ada_mod",
    )(cond, ada_w, ada_b.reshape(depth, 1, n))


def _mod_index(n_batch):
    return lambda b, i: (jnp.where(i == 0, n_batch, b), 0, 0)


def _norm_mod(x, g, shift, scale):
    ms = jnp.mean(x * x, axis=-1, keepdims=True)
    return x * lax.rsqrt(ms + RMS_EPS) * g * (1.0 + scale) + shift


def _inproj_even_kernel(x_ref, mod_ref, g_ref, w_ref, rkv_ref, lora_ref, gd_ref, conv_ref):
    m = mod_ref[0]
    h = _norm_mod(x_ref[0], g_ref[...], m[0:1], m[1:2]).astype(BF16)
    p = _dot(h, w_ref[...])
    rkv_ref[0] = p[:, 0:1536]
    lora_ref[0] = p[:, 1536:1792]
    gd_ref[0] = p[:, 1792:1920]
    conv_ref[0] = p[:, 1920:3456]


def inproj_even(x, modl, g, w):
    nb, s, d = x.shape
    tm = TOKEN_TILE
    tok = lambda n: pl.BlockSpec((1, tm, n), lambda b, i: (b, i, 0))
    return pl.pallas_call(
        _inproj_even_kernel,
        grid=(nb, s // tm),
        in_specs=[tok(d),
                  pl.BlockSpec((1, 8, d), _mod_index(nb)),
                  pl.BlockSpec((1, d), lambda b, i: (0, 0)),
                  pl.BlockSpec((d, EVEN_PROJ), lambda b, i: (0, 0))],
        out_specs=[tok(1536), tok(256), tok(128), tok(1536)],
        out_shape=[jax.ShapeDtypeStruct((nb, s, 1536), F32),
                   jax.ShapeDtypeStruct((nb, s, 256), F32),
                   jax.ShapeDtypeStruct((nb, s, 128), F32),
                   jax.ShapeDtypeStruct((nb, s, 1536), F32)],
        compiler_params=_cp(("parallel", "parallel")),
        name="inproj_even",
    )(x, modl, g, w)


def _rope(x, cos, sin_signed):
    lane = lax.broadcasted_iota(jnp.int32, x.shape, 1)
    partner = jnp.where((lane % 32) < 16, pltpu.roll(x, LANES - 16, 1), pltpu.roll(x, 16, 1))
    return x * cos + partner * sin_signed


def _inproj_odd_kernel(x_ref, mod_ref, g_ref, w_ref, cos_ref, sin_ref,
                       dq_ref, dk_ref, dv_ref, nq_ref, nk_ref, nv_ref):
    m = mod_ref[0]
    h = _norm_mod(x_ref[0], g_ref[...], m[0:1], m[1:2]).astype(BF16)
    p = _dot(h, w_ref[...])
    cos = cos_ref[...]
    sin = sin_ref[...]
    for out_ref, base in ((dq_ref, 0), (dk_ref, 512)):
        for c in range(4):
            blk = p[:, base + c * LANES: base + (c + 1) * LANES]
            out_ref[0, :, c * LANES:(c + 1) * LANES] = _rope(blk, cos, sin).astype(BF16)
    dv_ref[0] = p[:, 1024:1536].astype(BF16)
    nq_ref[0] = p[:, 1536:2048].astype(BF16)
    nk_ref[0] = p[:, 2048:2560].astype(BF16)
    nv_ref[0] = p[:, 2560:3072].astype(BF16)


def inproj_odd(x, modl, g, w, cos_t, sin_t):
    nb, s, d = x.shape
    tm = TOKEN_TILE
    tok = lambda n: pl.BlockSpec((1, tm, n), lambda b, i: (b, i, 0))
    tab = pl.BlockSpec((tm, LANES), lambda b, i: (i, 0))
    return pl.pallas_call(
        _inproj_odd_kernel,
        grid=(nb, s // tm),
        in_specs=[tok(d),
                  pl.BlockSpec((1, 8, d), _mod_index(nb)),
                  pl.BlockSpec((1, d), lambda b, i: (0, 0)),
                  pl.BlockSpec((d, ODD_PROJ), lambda b, i: (0, 0)),
                  tab, tab],
        out_specs=[tok(512)] * 6,
        out_shape=[jax.ShapeDtypeStruct((nb, s, 512), BF16)] * 6,
        compiler_params=_cp(("parallel", "parallel")),
        name="inproj_odd",
    )(x, modl, g, w, cos_t, sin_t)


def rope_tables(seq):
    n_freq = DIFF_HEAD // 4
    inv_freq = ROPE_BASE ** (-jnp.arange(n_freq, dtype=F32) / n_freq)
    t = jnp.arange(seq, dtype=jnp.int32)
    pos = jnp.stack([t // GRID_W, t % GRID_W], axis=-1).astype(F32)
    ang = pos[:, :, None] * inv_freq
    cos, sin = jnp.cos(ang), jnp.sin(ang)
    cos64 = jnp.concatenate([cos[:, 0], cos[:, 0], cos[:, 1], cos[:, 1]], axis=-1)
    sin64 = jnp.concatenate([-sin[:, 0], sin[:, 0], -sin[:, 1], sin[:, 1]], axis=-1)
    cos_t = jnp.concatenate([jnp.ones((CTX_LEN, 64), F32), cos64], axis=0)
    sin_t = jnp.concatenate([jnp.zeros((CTX_LEN, 64), F32), sin64], axis=0)
    return jnp.tile(cos_t, (1, 2)), jnp.tile(sin_t, (1, 2))


def _prep_kernel(r_ref, k_ref, lora_ref, w0_ref, wup_ref, a0_ref, aup_ref, kk_w_ref, ka_ref, xt_ref):
    k = k_ref[0]
    r = r_ref[0]
    lora = lora_ref[0]
    tb = SCAN_TB
    ones = _head_ones(RWKV_WIDTH, RWKV_HEAD)
    kk = k * kk_w_ref[...]
    ss = _head_sum(kk * kk, ones)
    kk = kk * lax.rsqrt(jnp.maximum(ss, 1e-24))
    ti = lax.broadcasted_iota(jnp.int32, (tb, tb), 0)
    tj = lax.broadcasted_iota(jnp.int32, (tb, tb), 1)
    same_group = (ti // SCAN_GROUP) == (tj // SCAN_GROUP)
    upto = [jnp.tile(jnp.logical_and(same_group, cmp).astype(BF16), (1, 3))
            for cmp in (tj <= ti,
                        tj >= ti)]

    def put(d, vec, x):
        width = SCAN_PAIRS * RWKV_HEAD
        for hl in range(2):
            heads = [x[:, (2 * p + hl) * RWKV_HEAD:(2 * p + hl + 1) * RWKV_HEAD] for p in range(SCAN_PAIRS)]
            for term_i, term in enumerate(_split3(jnp.concatenate(heads, axis=1))):
                xt_ref[0, d, term_i, hl, :, vec * width:(vec + 1) * width] = term

    for d in range(2):
        wd = lora[:, d * 64:(d + 1) * 64]
        ad = lora[:, 128 + d * 64:128 + (d + 1) * 64]
        w = w0_ref[d:d + 1, :] + _dot_hi(jnp.tanh(wd), wup_ref[d])
        log_decay = -math.exp(-0.5) * jax.nn.sigmoid(w)
        a = jax.nn.sigmoid(a0_ref[d:d + 1, :] + _dot_hi(ad, aup_ref[d]))
        log_g = _dot(upto[d], jnp.concatenate(_split3(log_decay), axis=0))
        g = jnp.exp(log_g)
        g_inv = jnp.exp(-log_g)
        put(d, 0, kk * jnp.exp(log_g - log_decay))
        put(d, 1, kk * a * g_inv)
        put(d, 2, k * (1.0 + (a - 1.0) * ka_ref[...]) * g_inv)
        put(d, 3, r * g)
        put(d, 4, g)


def rwkv_prep(rkv, lora, w0, wup, a0, aup, k_k, k_a):
    nb, s, _ = rkv.shape
    tb = SCAN_TB
    c = RWKV_WIDTH
    full = lambda shape: pl.BlockSpec(shape, lambda b, i: (0,) * len(shape))
    return pl.pallas_call(
        _prep_kernel,
        grid=(nb, s // tb),
        in_specs=[pl.BlockSpec((1, tb, c), lambda b, i: (b, i, 0)),
                  pl.BlockSpec((1, tb, c), lambda b, i: (b, i, 1)),
                  pl.BlockSpec((1, tb, 256), lambda b, i: (b, i, 0)),
                  full((2, c)), full((2, DECAY_LORA, c)), full((2, c)), full((2, ICLR_LORA, c)),
                  full((1, c)), full((1, c))],
        out_specs=pl.BlockSpec((1, 2, 3, 2, tb, SCAN_ROWS), lambda b, i: (b, 0, 0, 0, i, 0)),
        out_shape=jax.ShapeDtypeStruct((nb, 2, 3, 2, s, SCAN_ROWS), BF16),
        compiler_params=_cp(("parallel", "parallel")),
        name="rwkv_prep",
    )(rkv, rkv, lora, w0, wup, a0, aup, k_k.reshape(1, c), k_a.reshape(1, c))


def scan_selectors():
    grp = SCAN_GROUP
    j = jnp.arange(3 * 2 * grp)
    j_head, j_step = (j // grp) % 2, j % grp
    col = jnp.arange(grp * LANES)
    col_step, col_head = col // LANES, (col % LANES) // RWKV_HEAD
    base = j_head[:, None] == col_head[None, :]
    expand = jnp.stack([base & (j_step[:, None] == col_step[None, :]),
                        base & (j_step[:, None] == grp - 1 - col_step[None, :])])
    lane_head = jnp.arange(LANES) // RWKV_HEAD
    base_l = j_head[:, None] == lane_head[None, :]
    last = jnp.stack([base_l & (j_step[:, None] == grp - 1), base_l & (j_step[:, None] == 0)])
    return expand.astype(BF16), last.astype(BF16)


def _dot_tn(a, b):
    return lax.dot_general(a, b, (((0,), (0,)), ((), ())), preferred_element_type=F32)


def _scan_kernel(xtf_ref, xtb_ref, vf_ref, vb_ref, expand_ref, last_ref, yf_ref, yb_ref,
                 t_scr, big_scr, end_scr):
    tb, grp = SCAN_TB, SCAN_GROUP
    j = pl.program_id(1)

    @pl.when(j == 0)
    def _():
        t_scr[...] = jnp.zeros_like(t_scr)

    xts = (xtf_ref, xtb_ref)
    vs = (vf_ref, vb_ref)
    outs = (yf_ref, yb_ref)
    step_row = lax.broadcasted_iota(jnp.int32, (grp, LANES), 0)

    def group(g, carry):
        off = g * grp
        for d in range(2):
            rows = pl.ds(pl.multiple_of(off if d == 0 else tb - grp - off, grp), grp)
            slab = jnp.concatenate([xts[d][0, term, hl, rows, :] for term in range(3) for hl in range(2)],
                                   axis=0)
            big_scr[d] = _dot_tn(slab[:, 0:SCAN_STEP_ROWS], expand_ref[d])
            end_scr[d] = _dot_tn(slab[:, SCAN_STEP_ROWS:SCAN_ROWS], last_ref[d])
        for d in range(2):
            base = pl.multiple_of(off if d == 0 else tb - grp - off, grp)
            for p in range(SCAN_PAIRS):
                c = d * SCAN_PAIRS + p
                cols = pl.ds(p * LANES, LANES)
                v_win = vs[d][0, pl.ds(base, grp), cols]
                st = t_scr[c]
                y_win = jnp.zeros((grp, LANES), F32)
                for s in range(grp):
                    i = s if d == 0 else grp - 1 - s
                    col = lambda vec: big_scr[d, (vec * SCAN_PAIRS + p) * RWKV_HEAD:
                                              (vec * SCAN_PAIRS + p + 1) * RWKV_HEAD,
                                              s * LANES:(s + 1) * LANES]
                    s_kk = jnp.sum(st * col(0), axis=0, keepdims=True)
                    st = st - col(1) * s_kk + col(2) * v_win[i:i + 1, :]
                    y = jnp.sum(st * col(3), axis=0, keepdims=True)
                    y_win = jnp.where(step_row == i, y, y_win)
                t_scr[c] = st * end_scr[d, p * RWKV_HEAD:(p + 1) * RWKV_HEAD, :]
                outs[d][0, pl.ds(base, grp), cols] = y_win
        return carry

    lax.fori_loop(0, SCAN_NG, group, 0)


def rwkv_scan(xt, v_src):
    nb = xt.shape[0]
    s = v_src.shape[1]
    c = RWKV_WIDTH
    tb = SCAN_TB
    nt = s // tb
    nct = CTX_LEN // tb

    def bwd(j):
        return jnp.where(j < nct, nct - 1 - j, nt - 1 - (j - nct))

    expand, last = scan_selectors()
    whole = pl.BlockSpec(memory_space=pltpu.VMEM)
    y_shape = jax.ShapeDtypeStruct((nb, s, c), F32)
    return pl.pallas_call(
        _scan_kernel,
        grid=(nb, nt),
        in_specs=[pl.BlockSpec((1, None, 3, 2, tb, SCAN_ROWS), lambda b, j: (b, 0, 0, 0, j, 0)),
                  pl.BlockSpec((1, None, 3, 2, tb, SCAN_ROWS), lambda b, j: (b, 1, 0, 0, bwd(j), 0)),
                  pl.BlockSpec((1, tb, c), lambda b, j: (b, j, 2)),
                  pl.BlockSpec((1, tb, c), lambda b, j: (b, bwd(j), 2)),
                  whole, whole],
        out_specs=[pl.BlockSpec((1, tb, c), lambda b, j: (b, j, 0)),
                   pl.BlockSpec((1, tb, c), lambda b, j: (b, bwd(j), 0))],
        out_shape=[y_shape, y_shape],
        scratch_shapes=[pltpu.VMEM((2 * SCAN_PAIRS, RWKV_HEAD, LANES), F32),
                        pltpu.VMEM((2, SCAN_STEP_ROWS, SCAN_GROUP * LANES), F32),
                        pltpu.VMEM((2, SCAN_PAIRS * RWKV_HEAD, LANES), F32)],
        compiler_params=_cp(("parallel", "arbitrary")),
        name="rwkv_scan",
    )(xt, xt, v_src, v_src, expand, last)


def _readout_kernel(yf_ref, yb_ref, r_ref, k_ref, v_ref, gd_ref,
                    cb_ref, cc_ref, cx_ref, pc_ref, px_ref, nc_ref, nx_ref,
                    gup_ref, rk_ref, lw_ref, lb_ref, cw_ref, o_ref):
    i = pl.program_id(1)
    n_tiles = pl.num_programs(1)
    ones = _head_ones(RWKV_WIDTH, RWKV_HEAD)
    y = yf_ref[0] + yb_ref[0]
    mean = _head_sum(y, ones) * (1.0 / RWKV_HEAD)
    yc = y - mean
    var = _head_sum(yc * yc, ones) * (1.0 / RWKV_HEAD)
    out = yc * lax.rsqrt(var + LNX_EPS) * lw_ref[...] + lb_ref[...]
    bonus = _head_sum(r_ref[0] * k_ref[0] * rk_ref[...], ones) * v_ref[0]
    gate = _dot_hi(jax.nn.sigmoid(gd_ref[0]), gup_ref[...])
    o_ref[0, :, 0:RWKV_WIDTH] = ((out + bonus) * gate).astype(BF16)

    u = cc_ref[0] * cx_ref[0]
    tm = u.shape[0]
    row = lax.broadcasted_iota(jnp.int32, u.shape, 0)
    has_prev = jnp.logical_and(i != 0, i != CTX_LEN // tm)
    has_next = jnp.logical_and(i != CTX_LEN // tm - 1, i != n_tiles - 1)
    u_prev_edge = jnp.where(has_prev, pc_ref[0, 7:8, :] * px_ref[0, 7:8, :], 0.0)
    u_next_edge = jnp.where(has_next, nc_ref[0, 0:1, :] * nx_ref[0, 0:1, :], 0.0)
    u_prev = jnp.where(row == 0, u_prev_edge, pltpu.roll(u, 1, 0))
    u_next = jnp.where(row == tm - 1, u_next_edge, pltpu.roll(u, tm - 1, 0))
    conv = u_prev * cw_ref[0:1, :] + u * cw_ref[1:2, :] + u_next * cw_ref[2:3, :]
    o_ref[0, :, RWKV_WIDTH:] = (cb_ref[0] * conv).astype(BF16)


def rwkv_readout(yf, yb, rkv, gd, conv, g_up, r_k, lnx_w, lnx_b, conv_w):
    nb, s, c = yf.shape
    tm = TOKEN_TILE
    n_tiles = s // tm
    per8 = tm // 8
    last8 = s // 8 - 1
    tok = lambda col, n=c: pl.BlockSpec((1, tm, n), lambda b, i: (b, i, col))
    prev = lambda col: pl.BlockSpec((1, 8, c), lambda b, i: (b, jnp.maximum(i * per8 - 1, 0), col))
    nxt = lambda col: pl.BlockSpec((1, 8, c), lambda b, i: (b, jnp.minimum((i + 1) * per8, last8), col))
    full = lambda shape: pl.BlockSpec(shape, lambda b, i: (0,) * len(shape))
    return pl.pallas_call(
        _readout_kernel,
        grid=(nb, n_tiles),
        in_specs=[tok(0), tok(0), tok(0), tok(1), tok(2), tok(0, 128),
                  tok(0), tok(1), tok(2), prev(1), prev(2), nxt(1), nxt(2),
                  full((GATE_LORA, c)), full((1, c)), full((1, c)), full((1, c)), full((3, c))],
        out_specs=pl.BlockSpec((1, tm, 2 * c), lambda b, i: (b, i, 0)),
        out_shape=jax.ShapeDtypeStruct((nb, s, 2 * c), BF16),
        compiler_params=_cp(("parallel", "parallel")),
        name="rwkv_readout",
    )(yf, yb, rkv, rkv, rkv, gd, conv, conv, conv, conv, conv, conv, conv,
      g_up, r_k.reshape(1, c), lnx_w.reshape(1, c), lnx_b.reshape(1, c), conv_w)


def _softmax_rows(s):
    m = jnp.max(s, axis=-1, keepdims=True)
    e = jnp.exp(s - m)
    return e / jnp.sum(e, axis=-1, keepdims=True)


def _diff_kernel(q_ref, k_ref, v_ref, lam_ref, sub_ref, o_ref, *, lam_init):
    i = pl.program_id(2)
    lam = lam_ref[...]
    lam_full = (jnp.exp(jnp.sum(lam[0:1] * lam[1:2], axis=1, keepdims=True))
                - jnp.exp(jnp.sum(lam[2:3] * lam[3:4], axis=1, keepdims=True)) + lam_init)
    q = q_ref[0] * (DIFF_HEAD ** -0.5)
    lane = lax.broadcasted_iota(jnp.int32, q.shape, 1)
    q1 = jnp.where(lane < DIFF_HEAD, q, jnp.zeros_like(q))
    q2 = jnp.where(lane < DIFF_HEAD, jnp.zeros_like(q), q)

    def attend(n_keys):
        k = k_ref[0, 0:n_keys, :]
        v = v_ref[0, 0:n_keys, :]

        scores = [_dot_nt(qm, k) for qm in (q1, q2)]
        expd = [jnp.exp(s - jnp.max(s, axis=-1, keepdims=True)) for s in scores]
        sm = [_dot(e.astype(BF16), v) / jnp.sum(e, axis=-1, keepdims=True) for e in expd]
        o = sm[0] - lam_full * sm[1]
        ms = jnp.mean(o * o, axis=-1, keepdims=True)
        o = o * lax.rsqrt(ms + SUBLN_EPS) * sub_ref[...] * (1.0 - lam_init)
        o_ref[0] = o.astype(BF16)

    @pl.when(i == 0)
    def _():
        attend(CTX_LEN)

    @pl.when(i != 0)
    def _():
        attend(k_ref.shape[1])


def diff_attention(dq, dk, dv, lam, subln, lam_init):
    nb, s, _ = dq.shape
    tm = TOKEN_TILE
    return pl.pallas_call(
        functools.partial(_diff_kernel, lam_init=lam_init),
        grid=(nb, DIFF_HEADS, s // tm),
        in_specs=[pl.BlockSpec((1, tm, LANES), lambda b, h, i: (b, i, h)),
                  pl.BlockSpec((1, s, LANES), lambda b, h, i: (b, 0, h)),
                  pl.BlockSpec((1, s, LANES), lambda b, h, i: (b, 0, h)),
                  pl.BlockSpec((4, DIFF_HEAD), lambda b, h, i: (0, 0)),
                  pl.BlockSpec((1, DIFF_VHEAD), lambda b, h, i: (0, 0))],
        out_specs=pl.BlockSpec((1, tm, LANES), lambda b, h, i: (b, i, h)),
        out_shape=jax.ShapeDtypeStruct((nb, s, DIFF_WIDTH), BF16),
        compiler_params=_cp(("parallel", "parallel", "parallel")),
        name="diff_attention",
    )(dq, dk, dv, lam, subln.reshape(1, DIFF_VHEAD))


def na_bias_table(rel_bias, rows):
    kr = min(NA_MAX_ROWS, rows)
    cols = jnp.arange(GRID_W)
    col_start = jnp.clip(cols - NA_COLS // 2, 0, GRID_W - NA_COLS)
    kc = jnp.arange(GRID_W)
    inside = (kc[None, :] >= col_start[:, None]) & (kc[None, :] < col_start[:, None] + NA_COLS)
    col_off = kc[None, :] - cols[:, None] + (NA_COLS - 1)
    row_off = jnp.arange(kr)[None, :] - jnp.arange(NA_MAX_ROWS)[:, None] + (NA_MAX_ROWS - 1)
    pick_row = (row_off[:, :, None] == jnp.arange(2 * NA_MAX_ROWS - 1)).astype(F32)
    pick_col = ((col_off[:, :, None] == jnp.arange(2 * NA_COLS - 1)) & inside[:, :, None]).astype(F32)
    b = jnp.einsum('hro,pjr,cko->hpcjk', rel_bias.astype(F32), pick_row, pick_col, precision=HI)
    b = jnp.where(inside[None, None, :, None, :], b, NEG_BIG)
    return b.reshape(rel_bias.shape[0], NA_MAX_ROWS, GRID_W, kr * GRID_W)


def _na_kernel(q_ref, k_ref, v_ref, bias_ref, o_ref, *, rows):
    kr = min(NA_MAX_ROWS, rows)
    scale = NA_HEAD ** -0.5
    w = GRID_W
    lane_q = lax.broadcasted_iota(jnp.int32, (2 * w, LANES), 1)
    row_q = lax.broadcasted_iota(jnp.int32, (2 * w, LANES), 0)
    own = (lane_q < NA_HEAD) == (row_q < w)
    first_head = lax.broadcasted_iota(jnp.int32, (w, LANES), 1) < NA_HEAD
    k_ctx = k_ref[0, 0:CTX_LEN, :]
    v_ctx = v_ref[0, 0:CTX_LEN, :]

    def two_heads(q):
        q2 = jnp.concatenate([q, q], axis=0)
        return jnp.where(own, q2, jnp.zeros_like(q2))

    def merge(o):
        return jnp.where(first_head, o[0:w], o[w:2 * w])

    def body(it, carry):
        todo = []
        for u in range(NA_ROWS_PER_ITER):
            r = it * NA_ROWS_PER_ITER + u
            r_start = jnp.clip(r - kr // 2, 0, rows - kr)
            q = two_heads(q_ref[0, pl.ds(pl.multiple_of(CTX_LEN + r * w, w), w), :])
            base = pl.multiple_of(CTX_LEN + r_start * w, w)
            place = r - r_start
            bias = jnp.concatenate([bias_ref[0, place], bias_ref[1, place]], axis=0)
            s_loc = _dot_nt(q, k_ref[0, pl.ds(base, kr * w), :]) * scale + bias
            s_ctx = _dot_nt(q, k_ctx) * scale
            todo.append((r, base, s_loc, s_ctx))
        probs = []
        for r, base, s_loc, s_ctx in todo:
            m = jnp.maximum(jnp.max(s_loc, axis=-1, keepdims=True), jnp.max(s_ctx, axis=-1, keepdims=True))
            e_loc = jnp.exp(s_loc - m)
            e_ctx = jnp.exp(s_ctx - m)
            z = jnp.sum(e_loc, axis=-1, keepdims=True) + jnp.sum(e_ctx, axis=-1, keepdims=True)
            probs.append((r, base, (e_loc / z).astype(BF16), (e_ctx / z).astype(BF16)))
        for r, base, p_loc, p_ctx in probs:
            o = _dot(p_ctx, v_ctx) + _dot(p_loc, v_ref[0, pl.ds(base, kr * w), :])
            o_ref[0, pl.ds(pl.multiple_of(CTX_LEN + r * w, w), w), :] = merge(o).astype(BF16)
        return carry

    lax.fori_loop(0, rows // NA_ROWS_PER_ITER, body, 0)

    blocks = range(CTX_LEN // w)
    ctx_scores = [_dot_nt(two_heads(q_ref[0, blk * w:(blk + 1) * w, :]), k_ctx) * scale for blk in blocks]
    ctx_probs = [_softmax_rows(s).astype(BF16) for s in ctx_scores]
    for blk, p in zip(blocks, ctx_probs):
        o_ref[0, blk * w:(blk + 1) * w, :] = merge(_dot(p, v_ctx)).astype(BF16)


def na_attention(nq, nk, nv, bias_tab, rows):
    nb, s, _ = nq.shape
    kr = min(NA_MAX_ROWS, rows)
    seq = pl.BlockSpec((1, s, LANES), lambda b, p: (b, 0, p))
    return pl.pallas_call(
        functools.partial(_na_kernel, rows=rows),
        grid=(nb, NA_HEADS // 2),
        in_specs=[seq, seq, seq,
                  pl.BlockSpec((2, NA_MAX_ROWS, GRID_W, kr * GRID_W), lambda b, p: (p, 0, 0, 0))],
        out_specs=seq,
        out_shape=jax.ShapeDtypeStruct((nb, s, NA_WIDTH), BF16),
        compiler_params=_cp(("parallel", "parallel")),
        name="na_attention",
    )(nq, nk, nv, bias_tab)


def _route(logits):
    lane = lax.broadcasted_iota(jnp.int32, logits.shape, 1)
    big = jnp.int32(LANES)
    is_g = jnp.logical_and(lane >= N_EXPERTS, lane < N_EXPERTS + N_GROUPS)
    lg = jnp.where(is_g, logits, NEG_BIG)
    mg = jnp.max(lg, axis=-1, keepdims=True)
    zg = jnp.sum(jnp.where(is_g, jnp.exp(lg - mg), 0.0), axis=-1, keepdims=True)
    pg_top = 1.0 / zg
    g_sel = jnp.min(jnp.where(jnp.logical_and(is_g, lg == mg), lane, big), axis=-1, keepdims=True) - N_EXPERTS
    in_g = jnp.logical_and(lane < N_EXPERTS, (lane // EXPERTS_PER_GROUP) == g_sel)
    le = jnp.where(in_g, logits, NEG_BIG)
    m1 = jnp.max(le, axis=-1, keepdims=True)
    i1 = jnp.min(jnp.where(jnp.logical_and(in_g, le == m1), lane, big), axis=-1, keepdims=True)
    rest = jnp.logical_and(in_g, lane != i1)
    le2 = jnp.where(rest, logits, NEG_BIG)
    m2 = jnp.max(le2, axis=-1, keepdims=True)
    i2 = jnp.min(jnp.where(jnp.logical_and(rest, le2 == m2), lane, big), axis=-1, keepdims=True)
    e2 = jnp.exp(m2 - m1)
    w1 = pg_top / (1.0 + e2)
    w2 = pg_top * e2 / (1.0 + e2)
    return (jnp.where(lane == i1, w1, 0.0) + jnp.where(lane == i2, w2, 0.0)
            + jnp.where(lane == N_EXPERTS, g_sel.astype(F32), 0.0))


def _outproj_kernel(x_ref, oa_ref, ob_ref, mod_ref, g_ref, w_ref, wr_ref, br_ref,
                    xo_ref, hx_ref):
    m = mod_ref[0]
    half = oa_ref.shape[2]
    d = x_ref.shape[2]
    tm = x_ref.shape[1]
    parts = [slice(0, tm // 2), slice(tm // 2, tm)]
    ys = [_dot(oa_ref[0, p, :], w_ref[0:half, :]) + _dot(ob_ref[0, p, :], w_ref[half:, :]) for p in parts]
    xs = [x_ref[0, p, :] + m[2:3] * y for p, y in zip(parts, ys)]
    hs = [_norm_mod(x, g_ref[...], m[3:4], m[4:5]) for x in xs]
    logits = [_dot_hi(h, wr_ref[...]) + br_ref[...] for h in hs]
    for p, x, h, lg in zip(parts, xs, hs, logits):
        xo_ref[0, p, :] = x
        hx_ref[0, p, 0:d] = h
        hx_ref[0, p, d:] = _route(lg)


def outproj_route(x, oa, ob, ob_col, modl, g, w_out, w_route, b_route):
    nb, s, d = x.shape
    tm = TOKEN_TILE
    half = d // 2
    tok = lambda n, col=0: pl.BlockSpec((1, tm, n), lambda b, i: (b, i, col))
    full = lambda shape: pl.BlockSpec(shape, lambda b, i: (0,) * len(shape))
    return pl.pallas_call(
        _outproj_kernel,
        grid=(nb, s // tm),
        in_specs=[tok(d), tok(half), tok(half, ob_col),
                  pl.BlockSpec((1, 8, d), _mod_index(nb)),
                  full((1, d)), full((d, d)), full((d, LANES)), full((1, LANES))],
        out_specs=[tok(d), tok(d + LANES)],
        out_shape=[jax.ShapeDtypeStruct((nb, s, d), F32),
                   jax.ShapeDtypeStruct((nb, s, d + LANES), F32)],
        compiler_params=_cp(("parallel", "parallel")),
        name="outproj_route",
    )(x, oa, ob, modl, g, w_out, w_route, b_route)


def moe_group_tables(gid):
    n_tok = gid.shape[0]
    n_rows = n_tok + N_GROUPS * MOE_ROWS
    order = jnp.argsort(gid, stable=True).astype(jnp.int32)
    count = jnp.sum((gid[:, None] == jnp.arange(N_GROUPS)[None, :]).astype(jnp.int32), axis=0)
    padded = (count + MOE_ROWS - 1) // MOE_ROWS * MOE_ROWS
    end = jnp.cumsum(padded)
    first_sorted = jnp.cumsum(count) - count
    row = jnp.arange(n_rows, dtype=jnp.int32)
    group = jnp.minimum(jnp.sum((row[:, None] >= end[None, :]).astype(jnp.int32), axis=1), N_GROUPS - 1)
    local = row - (end - padded)[group]
    real = local < count[group]
    src = jnp.where(real, order[jnp.clip(first_sorted[group] + local, 0, n_tok - 1)], 0)
    real_before = first_sorted[group] + jnp.minimum(local, count[group])
    dst = jnp.where(real, src, n_tok + row - real_before)
    tile_group = group[::MOE_ROWS]
    return tile_group.astype(jnp.int32), src.astype(jnp.int32), dst.astype(jnp.int32)


def _moe_group_kernel(grp_ref, src_ref, dst_ref, hx_hbm, w1_ref, w3_ref, w2_ref, y_hbm,
                      hbuf, ybuf, in_sem, out_sem):
    i = pl.program_id(0)
    n = pl.num_programs(0)
    slot = lax.rem(i, 2)
    d = w1_ref.shape[1]

    def row_in(tile, sl, r):
        return pltpu.make_async_copy(hx_hbm.at[pl.ds(src_ref[tile * MOE_ROWS + r], 1)],
                                     hbuf.at[sl, pl.ds(r, 1)], in_sem.at[sl])

    def row_out(tile, sl, r):
        return pltpu.make_async_copy(ybuf.at[sl, pl.ds(r, 1)],
                                     y_hbm.at[pl.ds(dst_ref[tile * MOE_ROWS + r], 1)], out_sem.at[sl])

    def start_all(make, tile, sl):
        for r in range(MOE_ROWS):
            make(tile, sl, r).start()

    def wait_in(sl):
        pltpu.make_async_copy(hx_hbm.at[pl.ds(0, MOE_ROWS)], hbuf.at[sl], in_sem.at[sl]).wait()

    def wait_out(sl):
        pltpu.make_async_copy(ybuf.at[sl], y_hbm.at[pl.ds(0, MOE_ROWS)], out_sem.at[sl]).wait()

    @pl.when(i == 0)
    def _():
        start_all(row_in, 0, 0)

    start_all(row_in, jnp.minimum(i + 1, n - 1), 1 - slot)
    wait_in(slot)

    @pl.when(i >= 2)
    def _():
        wait_out(slot)

    rows = hbuf[slot]
    h = rows[:, 0:d].astype(BF16)
    comb = rows[:, d:]
    lane = lax.broadcasted_iota(jnp.int32, comb.shape, 1)
    first_expert = grp_ref[i] * EXPERTS_PER_GROUP
    y = jnp.zeros((MOE_ROWS, d), F32)
    for j in range(EXPERTS_PER_GROUP):
        a = _dot(h, w1_ref[j])
        a = a * jax.nn.sigmoid(a) * _dot(h, w3_ref[j])
        gate = jnp.sum(jnp.where(lane == first_expert + j, comb, 0.0), axis=-1, keepdims=True)
        y = y + gate * _dot(a.astype(BF16), w2_ref[j])
    ybuf[slot] = y
    start_all(row_out, i, slot)

    @pl.when(i == n - 1)
    def _():
        wait_in(1 - slot)
        wait_out(1 - slot)
        wait_out(slot)


def moe_grouped(hx, w1, w3, w2):
    n_tok, width = hx.shape
    d = width - LANES
    tile_group, src, dst = moe_group_tables(hx[:, d + N_EXPERTS].astype(jnp.int32))
    n_rows = src.shape[0]
    n_tiles = n_rows // MOE_ROWS
    assert n_tiles >= 2
    grouped = lambda arr: arr.reshape((N_GROUPS, EXPERTS_PER_GROUP) + arr.shape[1:])
    wspec = lambda a, b: pl.BlockSpec((None, EXPERTS_PER_GROUP, a, b), lambda i, grp, s_, d_: (grp[i], 0, 0, 0))
    return pl.pallas_call(
        _moe_group_kernel,
        grid_spec=pltpu.PrefetchScalarGridSpec(
            num_scalar_prefetch=3,
            grid=(n_tiles,),
            in_specs=[pl.BlockSpec(memory_space=pl.ANY),
                      wspec(d, D_EXPERT), wspec(d, D_EXPERT), wspec(D_EXPERT, d)],
            out_specs=pl.BlockSpec(memory_space=pl.ANY),
            scratch_shapes=[pltpu.VMEM((2, MOE_ROWS, width), F32),
                            pltpu.VMEM((2, MOE_ROWS, d), F32),
                            pltpu.SemaphoreType.DMA((2,)),
                            pltpu.SemaphoreType.DMA((2,))]),
        out_shape=jax.ShapeDtypeStruct((n_rows, d), F32),
        compiler_params=_cp(("arbitrary",)),
        name="moe_grouped",
    )(tile_group, src, dst, hx, grouped(w1), grouped(w3), grouped(w2))


def _moe_residual_kernel(x_ref, y_ref, mod_ref, o_ref):
    o_ref[0] = x_ref[0] + mod_ref[0][5:6] * y_ref[...]


def moe_residual(x, y, modl):
    nb, s, d = x.shape
    tm = TOKEN_TILE
    per_row = s // tm
    tok = pl.BlockSpec((1, tm, d), lambda b, i: (b, i, 0))
    return pl.pallas_call(
        _moe_residual_kernel,
        grid=(nb, per_row),
        in_specs=[tok,
                  pl.BlockSpec((tm, d), lambda b, i: (b * per_row + i, 0)),
                  pl.BlockSpec((1, 8, d), _mod_index(nb))],
        out_specs=tok,
        out_shape=jax.ShapeDtypeStruct((nb, s, d), F32),
        compiler_params=_cp(("parallel", "parallel")),
        name="moe_residual",
    )(x, y, modl)


def _final_kernel(x_ref, y_ref, mod_ref, g_ref, o_ref):
    x = x_ref[0] + mod_ref[0][5:6] * y_ref[...]
    ms = jnp.mean(x * x, axis=-1, keepdims=True)
    o_ref[0] = x * lax.rsqrt(ms + RMS_EPS) * g_ref[...]


def moe_residual_final_norm(x, y, modl, g):
    nb, s, d = x.shape
    tm = TOKEN_TILE
    per_row = s // tm
    skip = CTX_LEN // tm
    return pl.pallas_call(
        _final_kernel,
        grid=(nb, per_row - skip),
        in_specs=[pl.BlockSpec((1, tm, d), lambda b, i: (b, i + skip, 0)),
                  pl.BlockSpec((tm, d), lambda b, i: (b * per_row + i + skip, 0)),
                  pl.BlockSpec((1, 8, d), lambda b, i: (b, 0, 0)),
                  pl.BlockSpec((1, d), lambda b, i: (0, 0))],
        out_specs=pl.BlockSpec((1, tm, d), lambda b, i: (b, i, 0)),
        out_shape=jax.ShapeDtypeStruct((nb, s - CTX_LEN, d), F32),
        compiler_params=_cp(("parallel", "parallel")),
        name="final_norm",
    )(x, y, modl, g.reshape(1, d))


def kernel(x, c, ctx, c_ctx, ada_w, ada_b, norm_g, final_g, ev_w_in, ev_w_out, ev_decay_w0, ev_decay_up, ev_iclr_a0, ev_iclr_up, ev_gate_up, ev_k_k, ev_k_a, ev_r_k, ev_lnx_w, ev_lnx_b, ev_conv_w, od_w_in, od_w_out, od_lambda, od_subln, od_rel_bias, moe_wg, moe_bg, moe_we, moe_be, moe_w1, moe_w3, moe_w2):
    nb, seq, d = x.shape
    rows = seq // GRID_W
    depth = ada_w.shape[0]
    assert nb <= 15 and seq % TOKEN_TILE == 0 and ctx.shape[1] == CTX_LEN

    xs = jnp.concatenate([ctx, x], axis=1)
    cond = jnp.zeros((16, d), F32).at[:nb].set(c).at[nb].set(c_ctx)
    mod = ada_all(cond, ada_w, ada_b)
    mod = mod[:, :nb + 1].reshape(depth, nb + 1, 6, d)
    mod = jnp.concatenate([mod, jnp.zeros((depth, nb + 1, 2, d), F32)], axis=2)
    cos_t, sin_t = rope_tables(seq)

    for l in range(depth):
        i = l // 2
        modl = mod[l]
        g1 = norm_g[l, 0].reshape(1, d)
        g2 = norm_g[l, 1].reshape(1, d)
        if l % 2 == 0:
            rkv, lora, gd, conv = inproj_even(xs, modl, g1, ev_w_in[i].astype(BF16))
            xt = rwkv_prep(rkv, lora, ev_decay_w0[i], ev_decay_up[i], ev_iclr_a0[i],
                           ev_iclr_up[i], ev_k_k[i], ev_k_a[i])
            yf, yb = rwkv_scan(xt, rkv)
            o = rwkv_readout(yf, yb, rkv, gd, conv, ev_gate_up[i], ev_r_k[i], ev_lnx_w[i],
                             ev_lnx_b[i], ev_conv_w[i])
            oa, ob, ob_col = o, o, 1
            w_out = ev_w_out[i]
        else:
            dq, dk, dv, nq, nk, nv = inproj_odd(xs, modl, g1, od_w_in[i].astype(BF16), cos_t, sin_t)
            lam_init = 0.8 - 0.6 * math.exp(-0.3 * l)
            oa = diff_attention(dq, dk, dv, od_lambda[i], od_subln[i], lam_init)
            ob = na_attention(nq, nk, nv, na_bias_table(od_rel_bias[i], rows), rows)
            ob_col = 0
            w_out = od_w_out[i]
        w_route = jnp.concatenate(
            [moe_we[l], moe_wg[l], jnp.zeros((d, LANES - N_EXPERTS - N_GROUPS), F32)], axis=1)
        b_route = jnp.concatenate(
            [moe_be[l], moe_bg[l], jnp.zeros((LANES - N_EXPERTS - N_GROUPS,), F32)]).reshape(1, LANES)
        xs, hx = outproj_route(xs, oa, ob, ob_col, modl, g2, w_out.astype(BF16), w_route, b_route)
        y = moe_grouped(hx.reshape(nb * xs.shape[1], d + LANES), moe_w1[l].astype(BF16),
                        moe_w3[l].astype(BF16), moe_w2[l].astype(BF16))
        if l < depth - 1:
            xs = moe_residual(xs, y, modl)
    return moe_residual_final_norm(xs, y, mod[depth - 1], final_g)
```

```python
import functools
import math

import jax
import jax.numpy as jnp
from jax import lax
from jax.experimental import pallas as pl
from jax.experimental.pallas import tpu as pltpu

F32 = jnp.float32
BF16 = jnp.bfloat16
HI = lax.Precision.HIGHEST

D_MODEL = 1024
DEPTH = 4
GRID_W = 64
CTX_LEN = 256

RWKV_WIDTH = 512
RWKV_HEAD = 64
RWKV_HEADS = 8
DECAY_LORA = 64
ICLR_LORA = 64
GATE_LORA = 128
CONV_WIDTH = 512
LNX_EPS = 64e-5

DIFF_HEADS = 4
DIFF_HEAD = 64
DIFF_VHEAD = 128
DIFF_WIDTH = 512
NA_HEADS = 8
NA_HEAD = 64
NA_WIDTH = 512
NA_MAX_ROWS = 8
NA_COLS = 16
ROPE_BASE = 10000.0
SUBLN_EPS = 1e-5

N_GROUPS = 4
EXPERTS_PER_GROUP = 4
N_EXPERTS = 16
D_EXPERT = 512
RMS_EPS = 1e-6

EVEN_PROJ = 3456
ODD_PROJ = 3072

LANES = 128
TOKEN_TILE = 256
MOE_ROWS = 256
NA_ROWS_PER_ITER = 4
SCAN_TB = 128
SCAN_GROUP = 32
SCAN_VECS = 5
SCAN_PAIRS = RWKV_HEADS // 2
SCAN_ROWS = SCAN_VECS * SCAN_PAIRS * RWKV_HEAD
SCAN_STEP_ROWS = (SCAN_VECS - 1) * SCAN_PAIRS * RWKV_HEAD
SCAN_NG = SCAN_TB // SCAN_GROUP
VMEM_LIMIT = 56 * 1024 * 1024
NEG_BIG = -1e30


def _cp(sem, vmem=VMEM_LIMIT):
    return pltpu.CompilerParams(dimension_semantics=sem, vmem_limit_bytes=vmem)


def _dot(a, b):
    return jnp.dot(a, b, preferred_element_type=F32)


def _dot_hi(a, b):
    return jnp.dot(a, b, preferred_element_type=F32, precision=HI)


def _dot_nt(a, b):
    return lax.dot_general(a, b, (((1,), (1,)), ((), ())), preferred_element_type=F32)


def _split3(x):
    hi = x.astype(BF16)
    r1 = x - hi.astype(F32)
    mid = r1.astype(BF16)
    lo = (r1 - mid.astype(F32)).astype(BF16)
    return hi, mid, lo


def _head_ones(width, head):
    i = (lax.broadcasted_iota(jnp.int32, (3 * width, width), 0) % width) // head
    j = lax.broadcasted_iota(jnp.int32, (3 * width, width), 1) // head
    return (i == j).astype(BF16)


def _head_sum(x, ones3):
    return _dot(jnp.concatenate(_split3(x), axis=1), ones3)


def _ada_kernel(cond_ref, w_ref, b_ref, o_ref):
    cond = cond_ref[...]
    s = cond * jax.nn.sigmoid(cond)
    o_ref[0] = _dot_hi(s, w_ref[0]) + b_ref[0]


def ada_all(cond, ada_w, ada_b):
    depth, d, n = ada_w.shape
    tn = 1536
    return pl.pallas_call(
        _ada_kernel,
        grid=(depth, n // tn),
        in_specs=[pl.BlockSpec((16, d), lambda l, j: (0, 0)),
                  pl.BlockSpec((1, d, tn), lambda l, j: (l, 0, j)),
                  pl.BlockSpec((1, 1, tn), lambda l, j: (l, 0, j))],
        out_specs=pl.BlockSpec((1, 16, tn), lambda l, j: (l, 0, j)),
        out_shape=jax.ShapeDtypeStruct((depth, 16, n), F32),
        compiler_params=_cp(("parallel", "parallel")),
        name="ada_mod",
    )(cond, ada_w, ada_b.reshape(depth, 1, n))


def _mod_index(n_batch):
    return lambda b, i: (jnp.where(i == 0, n_batch, b), 0, 0)


def _norm_mod(x, g, shift, scale):
    ms = jnp.mean(x * x, axis=-1, keepdims=True)
    return x * lax.rsqrt(ms + RMS_EPS) * g * (1.0 + scale) + shift


def _stream_in(refs, fused):
    if not fused:
        return refs[0][0], refs[1:]
    x = refs[0][0] + refs[2][0][5:6] * refs[1][...]
    refs[-1][0] = x
    return x, refs[3:-1]


def _resid_specs(resid, nb, s, d):
    if resid is None:
        return (), [], [], []
    tm = TOKEN_TILE
    per_row = s // tm
    y, modl_prev = resid
    tok = pl.BlockSpec((1, tm, d), lambda b, i: (b, i, 0))
    return ((y, modl_prev),
            [pl.BlockSpec((tm, d), lambda b, i: (b * per_row + i, 0)), pl.BlockSpec((1, 8, d), _mod_index(nb))],
            [tok], [jax.ShapeDtypeStruct((nb, s, d), F32)])


def _inproj_even_kernel(*refs, fused):
    x, (mod_ref, g_ref, w_ref, rkv_ref, lora_ref, gd_ref, conv_ref) = _stream_in(refs, fused)
    m = mod_ref[0]
    h = _norm_mod(x, g_ref[...], m[0:1], m[1:2]).astype(BF16)
    p = _dot(h, w_ref[...])
    rkv_ref[0] = p[:, 0:1536]
    lora_ref[0] = p[:, 1536:1792]
    gd_ref[0] = p[:, 1792:1920]
    conv_ref[0] = p[:, 1920:3456]


def inproj_even(x, modl, g, w, resid=None):
    nb, s, d = x.shape
    tm = TOKEN_TILE
    tok = lambda n: pl.BlockSpec((1, tm, n), lambda b, i: (b, i, 0))
    r_in, r_specs, r_out_specs, r_out_shapes = _resid_specs(resid, nb, s, d)
    return pl.pallas_call(
        functools.partial(_inproj_even_kernel, fused=resid is not None),
        grid=(nb, s // tm),
        in_specs=[tok(d)] + r_specs + [
                  pl.BlockSpec((1, 8, d), _mod_index(nb)),
                  pl.BlockSpec((1, d), lambda b, i: (0, 0)),
                  pl.BlockSpec((d, EVEN_PROJ), lambda b, i: (0, 0))],
        out_specs=[tok(1536), tok(256), tok(128), tok(1536)] + r_out_specs,
        out_shape=[jax.ShapeDtypeStruct((nb, s, 1536), F32),
                   jax.ShapeDtypeStruct((nb, s, 256), F32),
                   jax.ShapeDtypeStruct((nb, s, 128), F32),
                   jax.ShapeDtypeStruct((nb, s, 1536), F32)] + r_out_shapes,
        compiler_params=_cp(("parallel", "parallel")),
        name="inproj_even",
    )(x, *r_in, modl, g, w)


def _rope(x, cos, sin_signed):
    lane = lax.broadcasted_iota(jnp.int32, x.shape, 1)
    partner = jnp.where((lane % 32) < 16, pltpu.roll(x, LANES - 16, 1), pltpu.roll(x, 16, 1))
    return x * cos + partner * sin_signed


def _inproj_odd_kernel(*refs, fused):
    x, (mod_ref, g_ref, w_ref, cos_ref, sin_ref,
        dq_ref, dk_ref, dv_ref, nq_ref, nk_ref, nv_ref) = _stream_in(refs, fused)
    m = mod_ref[0]
    h = _norm_mod(x, g_ref[...], m[0:1], m[1:2]).astype(BF16)
    p = _dot(h, w_ref[...])
    cos = cos_ref[...]
    sin = sin_ref[...]
    for out_ref, base in ((dq_ref, 0), (dk_ref, 512)):
        for c in range(4):
            blk = p[:, base + c * LANES: base + (c + 1) * LANES]
            out_ref[0, :, c * LANES:(c + 1) * LANES] = _rope(blk, cos, sin).astype(BF16)
    dv_ref[0] = p[:, 1024:1536].astype(BF16)
    nq_ref[0] = p[:, 1536:2048].astype(BF16)
    nk_ref[0] = p[:, 2048:2560].astype(BF16)
    nv_ref[0] = p[:, 2560:3072].astype(BF16)


def inproj_odd(x, modl, g, w, cos_t, sin_t, resid=None):
    nb, s, d = x.shape
    tm = TOKEN_TILE
    tok = lambda n: pl.BlockSpec((1, tm, n), lambda b, i: (b, i, 0))
    tab = pl.BlockSpec((tm, LANES), lambda b, i: (i, 0))
    r_in, r_specs, r_out_specs, r_out_shapes = _resid_specs(resid, nb, s, d)
    return pl.pallas_call(
        functools.partial(_inproj_odd_kernel, fused=resid is not None),
        grid=(nb, s // tm),
        in_specs=[tok(d)] + r_specs + [
                  pl.BlockSpec((1, 8, d), _mod_index(nb)),
                  pl.BlockSpec((1, d), lambda b, i: (0, 0)),
                  pl.BlockSpec((d, ODD_PROJ), lambda b, i: (0, 0)),
                  tab, tab],
        out_specs=[tok(512)] * 6 + r_out_specs,
        out_shape=[jax.ShapeDtypeStruct((nb, s, 512), BF16)] * 6 + r_out_shapes,
        compiler_params=_cp(("parallel", "parallel")),
        name="inproj_odd",
    )(x, *r_in, modl, g, w, cos_t, sin_t)


def rope_tables(seq):
    n_freq = DIFF_HEAD // 4
    inv_freq = ROPE_BASE ** (-jnp.arange(n_freq, dtype=F32) / n_freq)
    t = jnp.arange(seq, dtype=jnp.int32)
    pos = jnp.stack([t // GRID_W, t % GRID_W], axis=-1).astype(F32)
    ang = pos[:, :, None] * inv_freq
    cos, sin = jnp.cos(ang), jnp.sin(ang)
    cos64 = jnp.concatenate([cos[:, 0], cos[:, 0], cos[:, 1], cos[:, 1]], axis=-1)
    sin64 = jnp.concatenate([-sin[:, 0], sin[:, 0], -sin[:, 1], sin[:, 1]], axis=-1)
    cos_t = jnp.concatenate([jnp.ones((CTX_LEN, 64), F32), cos64], axis=0)
    sin_t = jnp.concatenate([jnp.zeros((CTX_LEN, 64), F32), sin64], axis=0)
    return jnp.tile(cos_t, (1, 2)), jnp.tile(sin_t, (1, 2))


def _prep_kernel(r_ref, k_ref, lora_ref, w0_ref, wup_ref, a0_ref, aup_ref, kk_w_ref, ka_ref, xt_ref):
    k = k_ref[0]
    r = r_ref[0]
    lora = lora_ref[0]
    tb = SCAN_TB
    ones = _head_ones(RWKV_WIDTH, RWKV_HEAD)
    kk = k * kk_w_ref[...]
    ss = _head_sum(kk * kk, ones)
    kk = kk * lax.rsqrt(jnp.maximum(ss, 1e-24))
    ti = lax.broadcasted_iota(jnp.int32, (tb, tb), 0)
    tj = lax.broadcasted_iota(jnp.int32, (tb, tb), 1)
    same_group = (ti // SCAN_GROUP) == (tj // SCAN_GROUP)
    upto = [jnp.tile(jnp.logical_and(same_group, cmp).astype(BF16), (1, 3))
            for cmp in (tj <= ti,
                        tj >= ti)]

    def put(d, vec, x):
        width = SCAN_PAIRS * RWKV_HEAD
        for hl in range(2):
            heads = [x[:, (2 * p + hl) * RWKV_HEAD:(2 * p + hl + 1) * RWKV_HEAD] for p in range(SCAN_PAIRS)]
            for term_i, term in enumerate(_split3(jnp.concatenate(heads, axis=1))):
                xt_ref[0, d, term_i, hl, :, vec * width:(vec + 1) * width] = term

    for d in range(2):
        wd = lora[:, d * 64:(d + 1) * 64]
        ad = lora[:, 128 + d * 64:128 + (d + 1) * 64]
        w = w0_ref[d:d + 1, :] + _dot_hi(jnp.tanh(wd), wup_ref[d])
        log_decay = -math.exp(-0.5) * jax.nn.sigmoid(w)
        a = jax.nn.sigmoid(a0_ref[d:d + 1, :] + _dot_hi(ad, aup_ref[d]))
        log_g = _dot(upto[d], jnp.concatenate(_split3(log_decay), axis=0))
        g = jnp.exp(log_g)
        g_inv = jnp.exp(-log_g)
        put(d, 0, kk * jnp.exp(log_g - log_decay))
        put(d, 1, kk * a * g_inv)
        put(d, 2, k * (1.0 + (a - 1.0) * ka_ref[...]) * g_inv)
        put(d, 3, r * g)
        put(d, 4, g)


def rwkv_prep(rkv, lora, w0, wup, a0, aup, k_k, k_a):
    nb, s, _ = rkv.shape
    tb = SCAN_TB
    c = RWKV_WIDTH
    full = lambda shape: pl.BlockSpec(shape, lambda b, i: (0,) * len(shape))
    return pl.pallas_call(
        _prep_kernel,
        grid=(nb, s // tb),
        in_specs=[pl.BlockSpec((1, tb, c), lambda b, i: (b, i, 0)),
                  pl.BlockSpec((1, tb, c), lambda b, i: (b, i, 1)),
                  pl.BlockSpec((1, tb, 256), lambda b, i: (b, i, 0)),
                  full((2, c)), full((2, DECAY_LORA, c)), full((2, c)), full((2, ICLR_LORA, c)),
                  full((1, c)), full((1, c))],
        out_specs=pl.BlockSpec((1, 2, 3, 2, tb, SCAN_ROWS), lambda b, i: (b, 0, 0, 0, i, 0)),
        out_shape=jax.ShapeDtypeStruct((nb, 2, 3, 2, s, SCAN_ROWS), BF16),
        compiler_params=_cp(("parallel", "parallel")),
        name="rwkv_prep",
    )(rkv, rkv, lora, w0, wup, a0, aup, k_k.reshape(1, c), k_a.reshape(1, c))


def scan_selectors():
    grp = SCAN_GROUP
    j = jnp.arange(3 * 2 * grp)
    j_head, j_step = (j // grp) % 2, j % grp
    col = jnp.arange(grp * LANES)
    col_step, col_head = col // LANES, (col % LANES) // RWKV_HEAD
    base = j_head[:, None] == col_head[None, :]
    expand = jnp.stack([base & (j_step[:, None] == col_step[None, :]),
                        base & (j_step[:, None] == grp - 1 - col_step[None, :])])
    lane_head = jnp.arange(LANES) // RWKV_HEAD
    base_l = j_head[:, None] == lane_head[None, :]
    last = jnp.stack([base_l & (j_step[:, None] == grp - 1), base_l & (j_step[:, None] == 0)])
    return expand.astype(BF16), last.astype(BF16)


def _dot_tn(a, b):
    return lax.dot_general(a, b, (((0,), (0,)), ((), ())), preferred_element_type=F32)


def _scan_kernel(xtf_ref, xtb_ref, vf_ref, vb_ref, expand_ref, last_ref, yf_ref, yb_ref,
                 t_scr, big_scr, end_scr):
    tb, grp = SCAN_TB, SCAN_GROUP
    j = pl.program_id(1)

    @pl.when(j == 0)
    def _():
        t_scr[...] = jnp.zeros_like(t_scr)

    xts = (xtf_ref, xtb_ref)
    vs = (vf_ref, vb_ref)
    outs = (yf_ref, yb_ref)
    step_row = lax.broadcasted_iota(jnp.int32, (grp, LANES), 0)

    def group(g, carry):
        off = g * grp
        for d in range(2):
            rows = pl.ds(pl.multiple_of(off if d == 0 else tb - grp - off, grp), grp)
            slab = jnp.concatenate([xts[d][0, term, hl, rows, :] for term in range(3) for hl in range(2)],
                                   axis=0)
            big_scr[d] = _dot_tn(slab[:, 0:SCAN_STEP_ROWS], expand_ref[d])
            end_scr[d] = _dot_tn(slab[:, SCAN_STEP_ROWS:SCAN_ROWS], last_ref[d])
        for d in range(2):
            base = pl.multiple_of(off if d == 0 else tb - grp - off, grp)
            for p in range(SCAN_PAIRS):
                c = d * SCAN_PAIRS + p
                cols = pl.ds(p * LANES, LANES)
                v_win = vs[d][0, pl.ds(base, grp), cols]
                st = t_scr[c]
                y_win = jnp.zeros((grp, LANES), F32)
                for s in range(grp):
                    i = s if d == 0 else grp - 1 - s
                    col = lambda vec: big_scr[d, (vec * SCAN_PAIRS + p) * RWKV_HEAD:
                                              (vec * SCAN_PAIRS + p + 1) * RWKV_HEAD,
                                              s * LANES:(s + 1) * LANES]
                    s_kk = jnp.sum(st * col(0), axis=0, keepdims=True)
                    st = st - col(1) * s_kk + col(2) * v_win[i:i + 1, :]
                    y = jnp.sum(st * col(3), axis=0, keepdims=True)
                    y_win = jnp.where(step_row == i, y, y_win)
                t_scr[c] = st * end_scr[d, p * RWKV_HEAD:(p + 1) * RWKV_HEAD, :]
                outs[d][0, pl.ds(base, grp), cols] = y_win
        return carry

    lax.fori_loop(0, SCAN_NG, group, 0)


def rwkv_scan(xt, v_src):
    nb = xt.shape[0]
    s = v_src.shape[1]
    c = RWKV_WIDTH
    tb = SCAN_TB
    nt = s // tb
    nct = CTX_LEN // tb

    def bwd(j):
        return jnp.where(j < nct, nct - 1 - j, nt - 1 - (j - nct))

    expand, last = scan_selectors()
    whole = pl.BlockSpec(memory_space=pltpu.VMEM)
    y_shape = jax.ShapeDtypeStruct((nb, s, c), F32)
    return pl.pallas_call(
        _scan_kernel,
        grid=(nb, nt),
        in_specs=[pl.BlockSpec((1, None, 3, 2, tb, SCAN_ROWS), lambda b, j: (b, 0, 0, 0, j, 0)),
                  pl.BlockSpec((1, None, 3, 2, tb, SCAN_ROWS), lambda b, j: (b, 1, 0, 0, bwd(j), 0)),
                  pl.BlockSpec((1, tb, c), lambda b, j: (b, j, 2)),
                  pl.BlockSpec((1, tb, c), lambda b, j: (b, bwd(j), 2)),
                  whole, whole],
        out_specs=[pl.BlockSpec((1, tb, c), lambda b, j: (b, j, 0)),
                   pl.BlockSpec((1, tb, c), lambda b, j: (b, bwd(j), 0))],
        out_shape=[y_shape, y_shape],
        scratch_shapes=[pltpu.VMEM((2 * SCAN_PAIRS, RWKV_HEAD, LANES), F32),
                        pltpu.VMEM((2, SCAN_STEP_ROWS, SCAN_GROUP * LANES), F32),
                        pltpu.VMEM((2, SCAN_PAIRS * RWKV_HEAD, LANES), F32)],
        compiler_params=_cp(("parallel", "arbitrary")),
        name="rwkv_scan",
    )(xt, xt, v_src, v_src, expand, last)


def _readout_kernel(yf_ref, yb_ref, r_ref, k_ref, v_ref, gd_ref,
                    cb_ref, cc_ref, cx_ref, pc_ref, px_ref, nc_ref, nx_ref,
                    gup_ref, rk_ref, lw_ref, lb_ref, cw_ref, o_ref):
    i = pl.program_id(1)
    n_tiles = pl.num_programs(1)
    ones = _head_ones(RWKV_WIDTH, RWKV_HEAD)
    y = yf_ref[0] + yb_ref[0]
    mean = _head_sum(y, ones) * (1.0 / RWKV_HEAD)
    yc = y - mean
    var = _head_sum(yc * yc, ones) * (1.0 / RWKV_HEAD)
    out = yc * lax.rsqrt(var + LNX_EPS) * lw_ref[...] + lb_ref[...]
    bonus = _head_sum(r_ref[0] * k_ref[0] * rk_ref[...], ones) * v_ref[0]
    gate = _dot_hi(jax.nn.sigmoid(gd_ref[0]), gup_ref[...])
    o_ref[0, :, 0:RWKV_WIDTH] = ((out + bonus) * gate).astype(BF16)

    u = cc_ref[0] * cx_ref[0]
    tm = u.shape[0]
    row = lax.broadcasted_iota(jnp.int32, u.shape, 0)
    has_prev = jnp.logical_and(i != 0, i != CTX_LEN // tm)
    has_next = jnp.logical_and(i != CTX_LEN // tm - 1, i != n_tiles - 1)
    u_prev_edge = jnp.where(has_prev, pc_ref[0, 7:8, :] * px_ref[0, 7:8, :], 0.0)
    u_next_edge = jnp.where(has_next, nc_ref[0, 0:1, :] * nx_ref[0, 0:1, :], 0.0)
    u_prev = jnp.where(row == 0, u_prev_edge, pltpu.roll(u, 1, 0))
    u_next = jnp.where(row == tm - 1, u_next_edge, pltpu.roll(u, tm - 1, 0))
    conv = u_prev * cw_ref[0:1, :] + u * cw_ref[1:2, :] + u_next * cw_ref[2:3, :]
    o_ref[0, :, RWKV_WIDTH:] = (cb_ref[0] * conv).astype(BF16)


def rwkv_readout(yf, yb, rkv, gd, conv, g_up, r_k, lnx_w, lnx_b, conv_w):
    nb, s, c = yf.shape
    tm = TOKEN_TILE
    n_tiles = s // tm
    per8 = tm // 8
    last8 = s // 8 - 1
    tok = lambda col, n=c: pl.BlockSpec((1, tm, n), lambda b, i: (b, i, col))
    prev = lambda col: pl.BlockSpec((1, 8, c), lambda b, i: (b, jnp.maximum(i * per8 - 1, 0), col))
    nxt = lambda col: pl.BlockSpec((1, 8, c), lambda b, i: (b, jnp.minimum((i + 1) * per8, last8), col))
    full = lambda shape: pl.BlockSpec(shape, lambda b, i: (0,) * len(shape))
    return pl.pallas_call(
        _readout_kernel,
        grid=(nb, n_tiles),
        in_specs=[tok(0), tok(0), tok(0), tok(1), tok(2), tok(0, 128),
                  tok(0), tok(1), tok(2), prev(1), prev(2), nxt(1), nxt(2),
                  full((GATE_LORA, c)), full((1, c)), full((1, c)), full((1, c)), full((3, c))],
        out_specs=pl.BlockSpec((1, tm, 2 * c), lambda b, i: (b, i, 0)),
        out_shape=jax.ShapeDtypeStruct((nb, s, 2 * c), BF16),
        compiler_params=_cp(("parallel", "parallel")),
        name="rwkv_readout",
    )(yf, yb, rkv, rkv, rkv, gd, conv, conv, conv, conv, conv, conv, conv,
      g_up, r_k.reshape(1, c), lnx_w.reshape(1, c), lnx_b.reshape(1, c), conv_w)


def _softmax_rows(s):
    m = jnp.max(s, axis=-1, keepdims=True)
    e = jnp.exp(s - m)
    return e / jnp.sum(e, axis=-1, keepdims=True)


def _diff_kernel(q_ref, k_ref, v_ref, lam_ref, sub_ref, o_ref, *, lam_init):
    i = pl.program_id(2)
    lam = lam_ref[...]
    lam_full = (jnp.exp(jnp.sum(lam[0:1] * lam[1:2], axis=1, keepdims=True))
                - jnp.exp(jnp.sum(lam[2:3] * lam[3:4], axis=1, keepdims=True)) + lam_init)
    q = q_ref[0] * (DIFF_HEAD ** -0.5)
    lane = lax.broadcasted_iota(jnp.int32, q.shape, 1)
    q1 = jnp.where(lane < DIFF_HEAD, q, jnp.zeros_like(q))
    q2 = jnp.where(lane < DIFF_HEAD, jnp.zeros_like(q), q)

    def attend(n_keys):
        k = k_ref[0, 0:n_keys, :]
        v = v_ref[0, 0:n_keys, :]

        scores = [_dot_nt(qm, k) for qm in (q1, q2)]
        expd = [jnp.exp(s - jnp.max(s, axis=-1, keepdims=True)) for s in scores]
        sm = [_dot(e.astype(BF16), v) / jnp.sum(e, axis=-1, keepdims=True) for e in expd]
        o = sm[0] - lam_full * sm[1]
        ms = jnp.mean(o * o, axis=-1, keepdims=True)
        o = o * lax.rsqrt(ms + SUBLN_EPS) * sub_ref[...] * (1.0 - lam_init)
        o_ref[0] = o.astype(BF16)

    @pl.when(i == 0)
    def _():
        attend(CTX_LEN)

    @pl.when(i != 0)
    def _():
        attend(k_ref.shape[1])


def diff_attention(dq, dk, dv, lam, subln, lam_init):
    nb, s, _ = dq.shape
    tm = TOKEN_TILE
    return pl.pallas_call(
        functools.partial(_diff_kernel, lam_init=lam_init),
        grid=(nb, DIFF_HEADS, s // tm),
        in_specs=[pl.BlockSpec((1, tm, LANES), lambda b, h, i: (b, i, h)),
                  pl.BlockSpec((1, s, LANES), lambda b, h, i: (b, 0, h)),
                  pl.BlockSpec((1, s, LANES), lambda b, h, i: (b, 0, h)),
                  pl.BlockSpec((4, DIFF_HEAD), lambda b, h, i: (0, 0)),
                  pl.BlockSpec((1, DIFF_VHEAD), lambda b, h, i: (0, 0))],
        out_specs=pl.BlockSpec((1, tm, LANES), lambda b, h, i: (b, i, h)),
        out_shape=jax.ShapeDtypeStruct((nb, s, DIFF_WIDTH), BF16),
        compiler_params=_cp(("parallel", "parallel", "parallel")),
        name="diff_attention",
    )(dq, dk, dv, lam, subln.reshape(1, DIFF_VHEAD))


def na_bias_table(rel_bias, rows):
    kr = min(NA_MAX_ROWS, rows)
    cols = jnp.arange(GRID_W)
    col_start = jnp.clip(cols - NA_COLS // 2, 0, GRID_W - NA_COLS)
    kc = jnp.arange(GRID_W)
    inside = (kc[None, :] >= col_start[:, None]) & (kc[None, :] < col_start[:, None] + NA_COLS)
    col_off = kc[None, :] - cols[:, None] + (NA_COLS - 1)
    row_off = jnp.arange(kr)[None, :] - jnp.arange(NA_MAX_ROWS)[:, None] + (NA_MAX_ROWS - 1)
    pick_row = (row_off[:, :, None] == jnp.arange(2 * NA_MAX_ROWS - 1)).astype(F32)
    pick_col = ((col_off[:, :, None] == jnp.arange(2 * NA_COLS - 1)) & inside[:, :, None]).astype(F32)
    b = jnp.einsum('hro,pjr,cko->hpcjk', rel_bias.astype(F32), pick_row, pick_col, precision=HI)
    b = jnp.where(inside[None, None, :, None, :], b, NEG_BIG)
    return b.reshape(rel_bias.shape[0], NA_MAX_ROWS, GRID_W, kr * GRID_W)


def _na_kernel(q_ref, k_ref, v_ref, bias_ref, o_ref, *, rows):
    kr = min(NA_MAX_ROWS, rows)
    scale = NA_HEAD ** -0.5
    w = GRID_W
    lane_q = lax.broadcasted_iota(jnp.int32, (2 * w, LANES), 1)
    row_q = lax.broadcasted_iota(jnp.int32, (2 * w, LANES), 0)
    own = (lane_q < NA_HEAD) == (row_q < w)
    first_head = lax.broadcasted_iota(jnp.int32, (w, LANES), 1) < NA_HEAD
    k_ctx = k_ref[0, 0:CTX_LEN, :]
    v_ctx = v_ref[0, 0:CTX_LEN, :]

    def two_heads(q):
        q2 = jnp.concatenate([q, q], axis=0)
        return jnp.where(own, q2, jnp.zeros_like(q2))

    def merge(o):
        return jnp.where(first_head, o[0:w], o[w:2 * w])

    def body(it, carry):
        todo = []
        for u in range(NA_ROWS_PER_ITER):
            r = it * NA_ROWS_PER_ITER + u
            r_start = jnp.clip(r - kr // 2, 0, rows - kr)
            q = two_heads(q_ref[0, pl.ds(pl.multiple_of(CTX_LEN + r * w, w), w), :])
            base = pl.multiple_of(CTX_LEN + r_start * w, w)
            place = r - r_start
            bias = jnp.concatenate([bias_ref[0, place], bias_ref[1, place]], axis=0)
            s_loc = _dot_nt(q, k_ref[0, pl.ds(base, kr * w), :]) * scale + bias
            s_ctx = _dot_nt(q, k_ctx) * scale
            todo.append((r, base, s_loc, s_ctx))
        probs = []
        for r, base, s_loc, s_ctx in todo:
            m = jnp.maximum(jnp.max(s_loc, axis=-1, keepdims=True), jnp.max(s_ctx, axis=-1, keepdims=True))
            e_loc = jnp.exp(s_loc - m)
            e_ctx = jnp.exp(s_ctx - m)
            z = jnp.sum(e_loc, axis=-1, keepdims=True) + jnp.sum(e_ctx, axis=-1, keepdims=True)
            probs.append((r, base, (e_loc / z).astype(BF16), (e_ctx / z).astype(BF16)))
        for r, base, p_loc, p_ctx in probs:
            o = _dot(p_ctx, v_ctx) + _dot(p_loc, v_ref[0, pl.ds(base, kr * w), :])
            o_ref[0, pl.ds(pl.multiple_of(CTX_LEN + r * w, w), w), :] = merge(o).astype(BF16)
        return carry

    lax.fori_loop(0, rows // NA_ROWS_PER_ITER, body, 0)

    blocks = range(CTX_LEN // w)
    ctx_scores = [_dot_nt(two_heads(q_ref[0, blk * w:(blk + 1) * w, :]), k_ctx) * scale for blk in blocks]
    ctx_probs = [_softmax_rows(s).astype(BF16) for s in ctx_scores]
    for blk, p in zip(blocks, ctx_probs):
        o_ref[0, blk * w:(blk + 1) * w, :] = merge(_dot(p, v_ctx)).astype(BF16)


def na_attention(nq, nk, nv, bias_tab, rows):
    nb, s, _ = nq.shape
    kr = min(NA_MAX_ROWS, rows)
    seq = pl.BlockSpec((1, s, LANES), lambda b, p: (b, 0, p))
    return pl.pallas_call(
        functools.partial(_na_kernel, rows=rows),
        grid=(nb, NA_HEADS // 2),
        in_specs=[seq, seq, seq,
                  pl.BlockSpec((2, NA_MAX_ROWS, GRID_W, kr * GRID_W), lambda b, p: (p, 0, 0, 0))],
        out_specs=seq,
        out_shape=jax.ShapeDtypeStruct((nb, s, NA_WIDTH), BF16),
        compiler_params=_cp(("parallel", "parallel")),
        name="na_attention",
    )(nq, nk, nv, bias_tab)


def _route(logits):
    lane = lax.broadcasted_iota(jnp.int32, logits.shape, 1)
    big = jnp.int32(LANES)
    is_g = jnp.logical_and(lane >= N_EXPERTS, lane < N_EXPERTS + N_GROUPS)
    lg = jnp.where(is_g, logits, NEG_BIG)
    mg = jnp.max(lg, axis=-1, keepdims=True)
    zg = jnp.sum(jnp.where(is_g, jnp.exp(lg - mg), 0.0), axis=-1, keepdims=True)
    pg_top = 1.0 / zg
    g_sel = jnp.min(jnp.where(jnp.logical_and(is_g, lg == mg), lane, big), axis=-1, keepdims=True) - N_EXPERTS
    in_g = jnp.logical_and(lane < N_EXPERTS, (lane // EXPERTS_PER_GROUP) == g_sel)
    le = jnp.where(in_g, logits, NEG_BIG)
    m1 = jnp.max(le, axis=-1, keepdims=True)
    i1 = jnp.min(jnp.where(jnp.logical_and(in_g, le == m1), lane, big), axis=-1, keepdims=True)
    rest = jnp.logical_and(in_g, lane != i1)
    le2 = jnp.where(rest, logits, NEG_BIG)
    m2 = jnp.max(le2, axis=-1, keepdims=True)
    i2 = jnp.min(jnp.where(jnp.logical_and(rest, le2 == m2), lane, big), axis=-1, keepdims=True)
    e2 = jnp.exp(m2 - m1)
    w1 = pg_top / (1.0 + e2)
    w2 = pg_top * e2 / (1.0 + e2)
    return (jnp.where(lane == i1, w1, 0.0) + jnp.where(lane == i2, w2, 0.0)
            + jnp.where(lane == N_EXPERTS, g_sel.astype(F32), 0.0))


def _outproj_kernel(x_ref, oa_ref, ob_ref, mod_ref, g_ref, w_ref, wr_ref, br_ref,
                    xo_ref, hx_ref):
    m = mod_ref[0]
    half = oa_ref.shape[2]
    d = x_ref.shape[2]
    tm = x_ref.shape[1]
    parts = [slice(0, tm // 2), slice(tm // 2, tm)]
    ys = [_dot(oa_ref[0, p, :], w_ref[0:half, :]) + _dot(ob_ref[0, p, :], w_ref[half:, :]) for p in parts]
    xs = [x_ref[0, p, :] + m[2:3] * y for p, y in zip(parts, ys)]
    hs = [_norm_mod(x, g_ref[...], m[3:4], m[4:5]) for x in xs]
    logits = [_dot_hi(h, wr_ref[...]) + br_ref[...] for h in hs]
    for p, x, h, lg in zip(parts, xs, hs, logits):
        xo_ref[0, p, :] = x
        hx_ref[0, p, 0:d] = h
        hx_ref[0, p, d:] = _route(lg)


def outproj_route(x, oa, ob, ob_col, modl, g, w_out, w_route, b_route):
    nb, s, d = x.shape
    tm = TOKEN_TILE
    half = d // 2
    tok = lambda n, col=0: pl.BlockSpec((1, tm, n), lambda b, i: (b, i, col))
    full = lambda shape: pl.BlockSpec(shape, lambda b, i: (0,) * len(shape))
    return pl.pallas_call(
        _outproj_kernel,
        grid=(nb, s // tm),
        in_specs=[tok(d), tok(half), tok(half, ob_col),
                  pl.BlockSpec((1, 8, d), _mod_index(nb)),
                  full((1, d)), full((d, d)), full((d, LANES)), full((1, LANES))],
        out_specs=[tok(d), tok(d + LANES)],
        out_shape=[jax.ShapeDtypeStruct((nb, s, d), F32),
                   jax.ShapeDtypeStruct((nb, s, d + LANES), F32)],
        compiler_params=_cp(("parallel", "parallel")),
        name="outproj_route",
    )(x, oa, ob, modl, g, w_out, w_route, b_route)


def moe_group_tables(gid):
    n_tok = gid.shape[0]
    n_rows = n_tok + N_GROUPS * MOE_ROWS
    order = jnp.argsort(gid, stable=True).astype(jnp.int32)
    count = jnp.sum((gid[:, None] == jnp.arange(N_GROUPS)[None, :]).astype(jnp.int32), axis=0)
    padded = (count + MOE_ROWS - 1) // MOE_ROWS * MOE_ROWS
    end = jnp.cumsum(padded)
    first_sorted = jnp.cumsum(count) - count
    row = jnp.arange(n_rows, dtype=jnp.int32)
    group = jnp.minimum(jnp.sum((row[:, None] >= end[None, :]).astype(jnp.int32), axis=1), N_GROUPS - 1)
    local = row - (end - padded)[group]
    real = local < count[group]
    src = jnp.where(real, order[jnp.clip(first_sorted[group] + local, 0, n_tok - 1)], 0)
    real_before = first_sorted[group] + jnp.minimum(local, count[group])
    dst = jnp.where(real, src, n_tok + row - real_before)
    tile_group = group[::MOE_ROWS]
    return tile_group.astype(jnp.int32), src.astype(jnp.int32), dst.astype(jnp.int32)


def _moe_group_kernel(grp_ref, src_ref, dst_ref, hx_hbm, w1_ref, w3_ref, w2_ref, y_hbm,
                      hbuf, ybuf, in_sem, out_sem):
    i = pl.program_id(0)
    n = pl.num_programs(0)
    slot = lax.rem(i, 2)
    d = w1_ref.shape[1]

    def row_in(tile, sl, r):
        return pltpu.make_async_copy(hx_hbm.at[pl.ds(src_ref[tile * MOE_ROWS + r], 1)],
                                     hbuf.at[sl, pl.ds(r, 1)], in_sem.at[sl])

    def row_out(tile, sl, r):
        return pltpu.make_async_copy(ybuf.at[sl, pl.ds(r, 1)],
                                     y_hbm.at[pl.ds(dst_ref[tile * MOE_ROWS + r], 1)], out_sem.at[sl])

    def start_all(make, tile, sl):
        for r in range(MOE_ROWS):
            make(tile, sl, r).start(priority=r % 2)

    def wait_in(sl):
        pltpu.make_async_copy(hx_hbm.at[pl.ds(0, MOE_ROWS)], hbuf.at[sl], in_sem.at[sl]).wait()

    def wait_out(sl):
        pltpu.make_async_copy(ybuf.at[sl], y_hbm.at[pl.ds(0, MOE_ROWS)], out_sem.at[sl]).wait()

    @pl.when(i == 0)
    def _():
        start_all(row_in, 0, 0)

    start_all(row_in, jnp.minimum(i + 1, n - 1), 1 - slot)
    wait_in(slot)

    @pl.when(i >= 2)
    def _():
        wait_out(slot)

    rows = hbuf[slot]
    h = rows[:, 0:d].astype(BF16)
    comb = rows[:, d:]
    lane = lax.broadcasted_iota(jnp.int32, comb.shape, 1)
    first_expert = grp_ref[i] * EXPERTS_PER_GROUP
    y = jnp.zeros((MOE_ROWS, d), F32)
    for j in range(EXPERTS_PER_GROUP):
        a = _dot(h, w1_ref[j])
        a = a * jax.nn.sigmoid(a) * _dot(h, w3_ref[j])
        gate = jnp.sum(jnp.where(lane == first_expert + j, comb, 0.0), axis=-1, keepdims=True)
        y = y + gate * _dot(a.astype(BF16), w2_ref[j])
    ybuf[slot] = y
    start_all(row_out, i, slot)

    @pl.when(i == n - 1)
    def _():
        wait_in(1 - slot)
        wait_out(1 - slot)
        wait_out(slot)


def moe_grouped(hx, w1, w3, w2):
    n_tok, width = hx.shape
    d = width - LANES
    tile_group, src, dst = moe_group_tables(hx[:, d + N_EXPERTS].astype(jnp.int32))
    n_rows = src.shape[0]
    n_tiles = n_rows // MOE_ROWS
    assert n_tiles >= 2
    grouped = lambda arr: arr.reshape((N_GROUPS, EXPERTS_PER_GROUP) + arr.shape[1:])
    wspec = lambda a, b: pl.BlockSpec((None, EXPERTS_PER_GROUP, a, b), lambda i, grp, s_, d_: (grp[i], 0, 0, 0))
    return pl.pallas_call(
        _moe_group_kernel,
        grid_spec=pltpu.PrefetchScalarGridSpec(
            num_scalar_prefetch=3,
            grid=(n_tiles,),
            in_specs=[pl.BlockSpec(memory_space=pl.ANY),
                      wspec(d, D_EXPERT), wspec(d, D_EXPERT), wspec(D_EXPERT, d)],
            out_specs=pl.BlockSpec(memory_space=pl.ANY),
            scratch_shapes=[pltpu.VMEM((2, MOE_ROWS, width), F32),
                            pltpu.VMEM((2, MOE_ROWS, d), F32),
                            pltpu.SemaphoreType.DMA((2,)),
                            pltpu.SemaphoreType.DMA((2,))]),
        out_shape=jax.ShapeDtypeStruct((n_rows, d), F32),
        compiler_params=_cp(("arbitrary",)),
        name="moe_grouped",
    )(tile_group, src, dst, hx, grouped(w1), grouped(w3), grouped(w2))


def _final_kernel(x_ref, y_ref, mod_ref, g_ref, o_ref):
    x = x_ref[0] + mod_ref[0][5:6] * y_ref[...]
    ms = jnp.mean(x * x, axis=-1, keepdims=True)
    o_ref[0] = x * lax.rsqrt(ms + RMS_EPS) * g_ref[...]


def moe_residual_final_norm(x, y, modl, g):
    nb, s, d = x.shape
    tm = TOKEN_TILE
    per_row = s // tm
    skip = CTX_LEN // tm
    return pl.pallas_call(
        _final_kernel,
        grid=(nb, per_row - skip),
        in_specs=[pl.BlockSpec((1, tm, d), lambda b, i: (b, i + skip, 0)),
                  pl.BlockSpec((tm, d), lambda b, i: (b * per_row + i + skip, 0)),
                  pl.BlockSpec((1, 8, d), lambda b, i: (b, 0, 0)),
                  pl.BlockSpec((1, d), lambda b, i: (0, 0))],
        out_specs=pl.BlockSpec((1, tm, d), lambda b, i: (b, i, 0)),
        out_shape=jax.ShapeDtypeStruct((nb, s - CTX_LEN, d), F32),
        compiler_params=_cp(("parallel", "parallel")),
        name="final_norm",
    )(x, y, modl, g.reshape(1, d))


def kernel(x, c, ctx, c_ctx, ada_w, ada_b, norm_g, final_g, ev_w_in, ev_w_out, ev_decay_w0, ev_decay_up, ev_iclr_a0, ev_iclr_up, ev_gate_up, ev_k_k, ev_k_a, ev_r_k, ev_lnx_w, ev_lnx_b, ev_conv_w, od_w_in, od_w_out, od_lambda, od_subln, od_rel_bias, moe_wg, moe_bg, moe_we, moe_be, moe_w1, moe_w3, moe_w2):
    nb, seq, d = x.shape
    rows = seq // GRID_W
    depth = ada_w.shape[0]
    assert nb <= 15 and seq % TOKEN_TILE == 0 and ctx.shape[1] == CTX_LEN

    xs = jnp.concatenate([ctx, x], axis=1)
    cond = jnp.zeros((16, d), F32).at[:nb].set(c).at[nb].set(c_ctx)
    mod = ada_all(cond, ada_w, ada_b)
    mod = mod[:, :nb + 1].reshape(depth, nb + 1, 6, d)
    mod = jnp.concatenate([mod, jnp.zeros((depth, nb + 1, 2, d), F32)], axis=2)
    cos_t, sin_t = rope_tables(seq)

    for l in range(depth):
        i = l // 2
        modl = mod[l]
        g1 = norm_g[l, 0].reshape(1, d)
        g2 = norm_g[l, 1].reshape(1, d)
        resid = None if l == 0 else (y, mod[l - 1])
        if l % 2 == 0:
            rkv, lora, gd, conv, *new_xs = inproj_even(xs, modl, g1, ev_w_in[i].astype(BF16), resid)
            xt = rwkv_prep(rkv, lora, ev_decay_w0[i], ev_decay_up[i], ev_iclr_a0[i],
                           ev_iclr_up[i], ev_k_k[i], ev_k_a[i])
            yf, yb = rwkv_scan(xt, rkv)
            o = rwkv_readout(yf, yb, rkv, gd, conv, ev_gate_up[i], ev_r_k[i], ev_lnx_w[i],
                             ev_lnx_b[i], ev_conv_w[i])
            oa, ob, ob_col = o, o, 1
            w_out = ev_w_out[i]
        else:
            dq, dk, dv, nq, nk, nv, *new_xs = inproj_odd(xs, modl, g1, od_w_in[i].astype(BF16), cos_t, sin_t,
                                                         resid)
            lam_init = 0.8 - 0.6 * math.exp(-0.3 * l)
            oa = diff_attention(dq, dk, dv, od_lambda[i], od_subln[i], lam_init)
            ob = na_attention(nq, nk, nv, na_bias_table(od_rel_bias[i], rows), rows)
            ob_col = 0
            w_out = od_w_out[i]
        if new_xs:
            xs = new_xs[0]
        w_route = jnp.concatenate(
            [moe_we[l], moe_wg[l], jnp.zeros((d, LANES - N_EXPERTS - N_GROUPS), F32)], axis=1)
        b_route = jnp.concatenate(
            [moe_be[l], moe_bg[l], jnp.zeros((LANES - N_EXPERTS - N_GROUPS,), F32)]).reshape(1, LANES)
        xs, hx = outproj_route(xs, oa, ob, ob_col, modl, g2, w_out.astype(BF16), w_route, b_route)
        y = moe_grouped(hx.reshape(nb * xs.shape[1], d + LANES), moe_w1[l].astype(BF16),
                        moe_w3[l].astype(BF16), moe_w2[l].astype(BF16))
    return moe_residual_final_norm(xs, y, mod[depth - 1], final_g)
```
